```python
import math
import jax
import jax.numpy as jnp
from jax import lax
import numpy as np

D_MODEL = 1024
BATCH = 2
SEQ = 8192
DEPTH = 2
DEC_BATCH = 32
DEC_SEQ = 4
PAST_LEN = 16384
PAGE_SIZE = 128

HEAD_DIM = 64
N_HEADS_GROUP = 4
GROUP_W = N_HEADS_GROUP * HEAD_DIM
MIX_W = 4 * GROUP_W
N_ATT_HEADS = 3 * N_HEADS_GROUP
ATT_W = N_ATT_HEADS * HEAD_DIM
DIFF_HALF = HEAD_DIM // 2
MOBA_BLOCK = 256
MOBA_TOPK = 3
Q_BLOCK = 128
RWKV_DECAY_LORA = 64
RWKV_A_LORA = 64
RWKV_GATE_LORA = 128
RWKV_W = 3 * GROUP_W + RWKV_DECAY_LORA + RWKV_A_LORA + RWKV_GATE_LORA
IN_W = 3 * ATT_W + N_HEADS_GROUP + RWKV_W
N_MEM = 256
MEM_HEADS = 4
MEM_W = MEM_HEADS * HEAD_DIM
D_FF = 2816
N_EXPERTS = 8
TOP_K = 2
N_DENSE = (DEPTH + 1) // 2
N_MOE = DEPTH // 2
NORM_EPS = 1e-6
GN_EPS = 64e-5
NEG_INF = -1e30
FORGET_BIAS_INIT = 2.0

kernel_name = 'hybrid_parallel_groups_step'


def lambda_init(layer):
    return 0.8 - 0.6 * math.exp(-0.3 * layer)


def rmsnorm(x, g):
    xf = x.astype(jnp.float32)
    y = xf * lax.rsqrt(jnp.mean(xf * xf, axis=-1, keepdims=True) + NORM_EPS)
    return (y * g.astype(jnp.float32)).astype(x.dtype)


def masked_softmax(logits, mask):
    return jax.nn.softmax(jnp.where(mask, logits.astype(jnp.float32), NEG_INF), axis=-1)


def diff_attend(q, k, v, q_pos, lam, sub_g, lam_init):
    mask = jnp.arange(k.shape[1])[None, :] <= q_pos[:, None]
    scale = DIFF_HALF ** -0.5
    s1 = jnp.einsum('bqhd,bkhd->bhqk', q[..., :DIFF_HALF], k[..., :DIFF_HALF]) * scale
    s2 = jnp.einsum('bqhd,bkhd->bhqk', q[..., DIFF_HALF:], k[..., DIFF_HALF:]) * scale
    p = masked_softmax(s1, mask) - lam * masked_softmax(s2, mask)
    o = jnp.einsum('bhqk,bkhd->bqhd', p.astype(v.dtype), v)
    return rmsnorm(o, sub_g) * (1.0 - lam_init)


def fox_attend(q, k, v, q_pos, cq, ck):
    mask = jnp.arange(k.shape[1])[None, :] <= q_pos[:, None]
    s = jnp.einsum('bqhd,bkhd->bhqk', q, k).astype(jnp.float32) * (HEAD_DIM ** -0.5)
    s = s + jnp.transpose(cq, (0, 2, 1))[:, :, :, None] - jnp.transpose(ck, (0, 2, 1))[:, :, None, :]
    p = masked_softmax(s, mask)
    return jnp.einsum('bhqk,bkhd->bqhd', p.astype(v.dtype), v)


def _gather_blocks(blocks, sel):
    per_head = jax.vmap(lambda bl, ix: bl[ix], in_axes=(2, 1), out_axes=1)
    return jax.vmap(per_head)(blocks, sel)


def moba_attend(q, kb, vb, kmean, q_pos):
    nb = kb.shape[1]
    own = q_pos // MOBA_BLOCK
    gate = jnp.einsum('bqhd,bnhd->bqhn', q, kmean).astype(jnp.float32)
    fully_past = (jnp.arange(nb)[None, :] < own[:, None])[None, :, None, :]
    gate = jnp.where(fully_past, gate, NEG_INF)
    _, top_i = lax.top_k(gate, min(MOBA_TOPK, nb))
    own_b = jnp.broadcast_to(own[None, :, None, None], top_i.shape[:3] + (1,)).astype(top_i.dtype)
    sel = jnp.concatenate([top_i, own_b], axis=-1)
    blk_ok = jnp.concatenate([top_i < own_b, jnp.ones(own_b.shape, dtype=bool)], axis=-1)
    kg = _gather_blocks(kb, sel)
    vg = _gather_blocks(vb, sel)
    kpos = sel[..., None] * MOBA_BLOCK + jnp.arange(MOBA_BLOCK)
    mask = blk_ok[..., None] & (kpos <= q_pos[None, :, None, None, None])
    s = jnp.einsum('bqhd,bqhnkd->bqhnk', q, kg) * (HEAD_DIM ** -0.5)
    B, Q, H = q.shape[:3]
    p = masked_softmax(s.reshape(B, Q, H, -1), mask.reshape(B, Q, H, -1))
    return jnp.einsum('bqhj,bqhjd->bqhd', p.astype(vg.dtype), vg.reshape(B, Q, H, -1, HEAD_DIM))


def _split_q(a, axis):
    nb = a.shape[axis] // Q_BLOCK
    a = a.reshape(a.shape[:axis] + (nb, Q_BLOCK) + a.shape[axis + 1:])
    return jnp.moveaxis(a, axis, 0)


def sweep_queries(fn, qs, axes, T):
    if T > Q_BLOCK and T % Q_BLOCK == 0:
        blocks = tuple(_split_q(a, ax) for a, ax in zip(qs, axes))
        outs = lax.map(lambda b: fn(*b), blocks)
        res = []
        for o in outs:
            o = jnp.moveaxis(o, 0, 1)
            res.append(o.reshape((o.shape[0], T) + o.shape[3:]))
        return tuple(res)
    return fn(*qs)


def rwkv7_mix(pr, shift0, wkv0, p):
    B, T, _ = pr.shape
    G = N_HEADS_GROUP
    f32 = jnp.float32
    prev = jnp.concatenate([shift0[:, None, :].astype(pr.dtype), pr[:, :-1]], axis=1)
    xm = pr + (prev - pr) * p['rw_mu']
    c0 = 3 * GROUP_W
    r, k, v, wi, ai, gi = jnp.split(
        xm, [GROUP_W, 2 * GROUP_W, c0, c0 + RWKV_DECAY_LORA, c0 + RWKV_DECAY_LORA + RWKV_A_LORA], axis=-1)
    heads = lambda t: t.astype(f32).reshape(B, T, G, HEAD_DIM)
    w_log = -jax.nn.softplus(-(p['rw_w0'] + jnp.tanh(wi) @ p['rw_w2']).astype(f32)) - 0.5
    decay = heads(jnp.exp(-jnp.exp(w_log)))
    a = heads(jax.nn.sigmoid((p['rw_a0'] + ai @ p['rw_a2']).astype(f32)))
    g = jax.nn.sigmoid(gi) @ p['rw_g2']
    kk = heads(k * p['rw_kk'])
    kk = kk * lax.rsqrt(jnp.sum(kk * kk, axis=-1, keepdims=True) + 1e-12)
    ka = p['rw_ka'].astype(f32).reshape(G, HEAD_DIM)
    kh = heads(k) * (1.0 + (a - 1.0) * ka)
    vh = heads(v)
    rh = heads(r)

    def step(S, inp):
        r_t, w_t, k_t, v_t, kk_t, a_t = inp
        sa = jnp.einsum('bhij,bhj->bhi', S, -kk_t)
        S = (S * w_t[:, :, None, :] + sa[..., None] * (kk_t * a_t)[:, :, None, :]
             + v_t[..., None] * k_t[:, :, None, :])
        return S, jnp.einsum('bhij,bhj->bhi', S, r_t)

    tmajor = lambda t: jnp.moveaxis(t, 1, 0)
    S_T, o = lax.scan(step, wkv0.astype(f32), tuple(tmajor(t) for t in (rh, decay, kh, vh, kk, a)))
    o = jnp.moveaxis(o, 0, 1)
    mu = jnp.mean(o, axis=-1, keepdims=True)
    var = jnp.mean(jnp.square(o - mu), axis=-1, keepdims=True)
    o = ((o - mu) * lax.rsqrt(var + GN_EPS)).reshape(B, T, GROUP_W) * p['rw_ln_g'] + p['rw_ln_b']
    bonus = jnp.sum(rh * kh * p['rw_rk'].astype(f32), axis=-1, keepdims=True) * vh
    o = (o + bonus.reshape(B, T, GROUP_W)) * g
    return o.astype(pr.dtype), S_T.astype(wkv0.dtype), pr[:, -1]


def mixer_block(h, past_k, past_v, past_logf, wkv0, shift0, p, lam_init):
    B, T, _ = h.shape
    P = past_k.shape[1]
    L = P + T
    G = N_HEADS_GROUP
    proj = h @ p['w_in']
    q, k, v, fl, pr = jnp.split(proj, [ATT_W, 2 * ATT_W, 3 * ATT_W, 3 * ATT_W + G], axis=-1)
    q = q.reshape(B, T, N_ATT_HEADS, HEAD_DIM)
    k = k.reshape(B, T, N_ATT_HEADS, HEAD_DIM)
    v = v.reshape(B, T, N_ATT_HEADS, HEAD_DIM)
    logf = jax.nn.log_sigmoid((fl + p['b_f']).astype(jnp.float32))
    k_all = jnp.concatenate([past_k, k], axis=1)
    v_all = jnp.concatenate([past_v, v], axis=1)
    cum = jnp.cumsum(jnp.concatenate([past_logf.astype(jnp.float32), logf], axis=1), axis=1)
    kA, kB, kC = k_all[:, :, :G], k_all[:, :, G:2 * G], k_all[:, :, 2 * G:]
    vA, vB, vC = v_all[:, :, :G], v_all[:, :, G:2 * G], v_all[:, :, 2 * G:]
    n_blk = -(-L // MOBA_BLOCK)
    pad = ((0, 0), (0, n_blk * MOBA_BLOCK - L), (0, 0), (0, 0))
    kB_blk = jnp.pad(kB, pad).reshape(B, n_blk, MOBA_BLOCK, G, HEAD_DIM)
    vB_blk = jnp.pad(vB, pad).reshape(B, n_blk, MOBA_BLOCK, G, HEAD_DIM)
    kB_mean = jnp.mean(kB_blk.astype(jnp.float32), axis=2).astype(kB.dtype)
    lp = p['diff_lam']
    lam = jnp.exp(jnp.sum(lp[0] * lp[1])) - jnp.exp(jnp.sum(lp[2] * lp[3])) + lam_init
    q_pos = P + jnp.arange(T, dtype=jnp.int32)

    def attend(qa, qb, qc, cq, qp):
        return (diff_attend(qa, kA, vA, qp, lam, p['diff_g'], lam_init),
                moba_attend(qb, kB_blk, vB_blk, kB_mean, qp),
                fox_attend(qc, kC, vC, qp, cq, cum))

    oA, oB, oC = sweep_queries(
        attend, (q[:, :, :G], q[:, :, G:2 * G], q[:, :, 2 * G:], cum[:, P:], q_pos), (1, 1, 1, 1, 0), T)
    oD, wkv_T, shift_T = rwkv7_mix(pr, shift0, wkv0, p)
    o = jnp.concatenate([oA.reshape(B, T, GROUP_W), oB.reshape(B, T, GROUP_W),
                         oC.reshape(B, T, GROUP_W), oD], axis=-1)
    return o @ p['w_out'], k, v, logf.astype(h.dtype), wkv_T, shift_T


def memory_kv(mem, p):
    B, M, _ = mem.shape
    kv = rmsnorm(mem, p['mem_norm']) @ p['w_mem_kv']
    mk, mv = jnp.split(kv, 2, axis=-1)
    return mk.reshape(B, M, MEM_HEADS, HEAD_DIM), mv.reshape(B, M, MEM_HEADS, HEAD_DIM)


def memory_attend(h, mk, mv, p):
    B, T, _ = h.shape
    q = (h @ p['w_mem_q']).reshape(B, T, MEM_HEADS, HEAD_DIM)
    s = jnp.einsum('bthd,bmhd->bhtm', q, mk) * (HEAD_DIM ** -0.5)
    a = jax.nn.softmax(s.astype(jnp.float32), axis=-1)
    o = jnp.einsum('bhtm,bmhd->bthd', a.astype(mv.dtype), mv)
    return o.reshape(B, T, MEM_W) @ p['w_mem_o']


def swiglu(h, w_gu, w_d):
    g, u = jnp.split(h @ w_gu, 2, axis=-1)
    return (jax.nn.silu(g) * u) @ w_d


def moe_ffn(h, router, router_b, e_gu, e_d):
    shp = h.shape
    hf = h.reshape(-1, shp[-1])
    logits = (hf @ router + router_b).astype(jnp.float32)
    top_v, top_i = lax.top_k(logits, TOP_K)
    wts = jax.nn.softmax(top_v, axis=-1)
    gates = jnp.sum(jax.nn.one_hot(top_i, N_EXPERTS, dtype=jnp.float32) * wts[..., None], axis=1)
    y = jnp.zeros_like(hf)
    for e in range(N_EXPERTS):
        y = y + gates[:, e:e + 1].astype(hf.dtype) * swiglu(hf, e_gu[e], e_d[e])
    return y.reshape(shp)


def run_layer(x, past_k, past_v, past_logf, wkv0, shift0, mem_k, mem_v, p, lam_init, ffn):
    y, k, v, logf, wkv, shift = mixer_block(rmsnorm(x, p['norm_mix']), past_k, past_v, past_logf,
                                            wkv0, shift0, p, lam_init)
    x = x + y
    x = x + memory_attend(rmsnorm(x, p['norm_mem']), mem_k, mem_v, p)
    x = x + ffn(rmsnorm(x, p['norm_ffn']))
    return x, k, v, logf, wkv, shift


def setup_inputs(seed: int = 0) -> dict:
    key = jax.random.key(seed)
    ks = iter(jax.random.split(key, 64))
    f32 = jnp.float32

    def nrm(shape, scale=1.0):
        return scale * jax.random.normal(next(ks), shape, f32)

    def gain(shape):
        return 1.0 + 0.02 * jax.random.normal(next(ks), shape, f32)

    G = N_HEADS_GROUP
    n_pages = PAST_LEN // PAGE_SIZE
    n_used = DEC_BATCH * n_pages
    n_pool = n_used + max(1, n_used // 4)
    inp = {}
    inp['x_prompt'] = nrm((BATCH, SEQ, D_MODEL))
    inp['x_sample'] = nrm((DEC_BATCH, DEC_SEQ, D_MODEL))
    inp['cache_k'] = nrm((DEPTH, n_pool, PAGE_SIZE, N_ATT_HEADS, HEAD_DIM))
    inp['cache_v'] = nrm((DEPTH, n_pool, PAGE_SIZE, N_ATT_HEADS, HEAD_DIM))
    inp['cache_logf'] = jax.nn.log_sigmoid(FORGET_BIAS_INIT + nrm((DEPTH, n_pool, PAGE_SIZE, G), 0.5))
    inp['cache_mem_k'] = nrm((DEPTH, DEC_BATCH, N_MEM, MEM_HEADS, HEAD_DIM))
    inp['cache_mem_v'] = nrm((DEPTH, DEC_BATCH, N_MEM, MEM_HEADS, HEAD_DIM))
    inp['state_wkv'] = nrm((DEPTH, DEC_BATCH, G, HEAD_DIM, HEAD_DIM), 0.3)
    inp['state_shift'] = nrm((DEPTH, DEC_BATCH, RWKV_W))
    perm = jax.random.permutation(next(ks), n_pool)
    inp['page_table'] = perm[:n_used].astype(jnp.int32).reshape(DEC_BATCH, n_pages)
    inp['mem_prompt'] = nrm((BATCH, N_MEM, D_MODEL))
    inp['norm_mix'] = gain((DEPTH, D_MODEL))
    inp['w_in'] = nrm((DEPTH, D_MODEL, IN_W), D_MODEL ** -0.5)
    inp['b_f'] = FORGET_BIAS_INIT + nrm((DEPTH, G), 0.5)
    inp['diff_lam'] = nrm((DEPTH, 4, DIFF_HALF), 0.1)
    inp['diff_g'] = gain((DEPTH, HEAD_DIM))
    inp['rw_mu'] = jax.random.uniform(next(ks), (DEPTH, RWKV_W), f32)
    inp['rw_w0'] = jax.random.uniform(next(ks), (DEPTH, GROUP_W), f32, minval=-6.0, maxval=-1.0)
    inp['rw_w2'] = nrm((DEPTH, RWKV_DECAY_LORA, GROUP_W), 0.1)
    inp['rw_a0'] = nrm((DEPTH, GROUP_W), 0.1)
    inp['rw_a2'] = nrm((DEPTH, RWKV_A_LORA, GROUP_W), RWKV_A_LORA ** -0.5)
    inp['rw_g2'] = nrm((DEPTH, RWKV_GATE_LORA, GROUP_W), RWKV_GATE_LORA ** -0.5)
    inp['rw_kk'] = 0.85 + nrm((DEPTH, GROUP_W), 0.02)
    inp['rw_ka'] = 1.0 + nrm((DEPTH, GROUP_W), 0.02)
    inp['rw_rk'] = nrm((DEPTH, G, HEAD_DIM), 0.1)
    inp['rw_ln_g'] = gain((DEPTH, GROUP_W))
    inp['rw_ln_b'] = nrm((DEPTH, GROUP_W), 0.02)
    inp['w_out'] = nrm((DEPTH, MIX_W, D_MODEL), MIX_W ** -0.5)
    inp['norm_mem'] = gain((DEPTH, D_MODEL))
    inp['mem_norm'] = gain((DEPTH, D_MODEL))
    inp['w_mem_q'] = nrm((DEPTH, D_MODEL, MEM_W), D_MODEL ** -0.5)
    inp['w_mem_kv'] = nrm((DEPTH, D_MODEL, 2 * MEM_W), D_MODEL ** -0.5)
    inp['w_mem_o'] = nrm((DEPTH, MEM_W, D_MODEL), MEM_W ** -0.5)
    inp['norm_ffn'] = gain((DEPTH, D_MODEL))
    inp['ffn_gu'] = nrm((N_DENSE, D_MODEL, 2 * D_FF), D_MODEL ** -0.5)
    inp['ffn_down'] = nrm((N_DENSE, D_FF, D_MODEL), D_FF ** -0.5)
    inp['moe_router'] = nrm((N_MOE, D_MODEL, N_EXPERTS), D_MODEL ** -0.5)
    inp['moe_router_b'] = nrm((N_MOE, N_EXPERTS), 0.01)
    inp['moe_gu'] = nrm((N_MOE, N_EXPERTS, D_MODEL, 2 * D_FF), D_MODEL ** -0.5)
    inp['moe_down'] = nrm((N_MOE, N_EXPERTS, D_FF, D_MODEL), D_FF ** -0.5)
    inp['norm_final'] = gain((D_MODEL,))
    return inp


def reference(x_prompt, x_sample, cache_k, cache_v, cache_logf, cache_mem_k, cache_mem_v,
              state_wkv, state_shift, page_table, mem_prompt,
              norm_mix, w_in, b_f, diff_lam, diff_g, rw_mu, rw_w0, rw_w2, rw_a0, rw_a2, rw_g2,
              rw_kk, rw_ka, rw_rk, rw_ln_g, rw_ln_b, w_out, norm_mem, mem_norm, w_mem_q, w_mem_kv,
              w_mem_o, norm_ffn, ffn_gu, ffn_down, moe_router, moe_router_b, moe_gu, moe_down,
              norm_final):
    B = x_prompt.shape[0]
    DB = x_sample.shape[0]
    G = N_HEADS_GROUP
    past_len = page_table.shape[1] * PAGE_SIZE
    dt = x_prompt.dtype
    xp, xs = x_prompt, x_sample
    kp_l, vp_l, lfp_l, mkp_l, mvp_l, wkvp_l, shp_l = [], [], [], [], [], [], []
    ks_l, vs_l, lfs_l, wkvs_l, shs_l = [], [], [], [], []
    for l in range(DEPTH):
        p = {'norm_mix': norm_mix[l], 'w_in': w_in[l], 'b_f': b_f[l],
             'diff_lam': diff_lam[l], 'diff_g': diff_g[l],
             'rw_mu': rw_mu[l], 'rw_w0': rw_w0[l], 'rw_w2': rw_w2[l],
             'rw_a0': rw_a0[l], 'rw_a2': rw_a2[l], 'rw_g2': rw_g2[l],
             'rw_kk': rw_kk[l], 'rw_ka': rw_ka[l], 'rw_rk': rw_rk[l],
             'rw_ln_g': rw_ln_g[l], 'rw_ln_b': rw_ln_b[l], 'w_out': w_out[l],
             'norm_mem': norm_mem[l], 'mem_norm': mem_norm[l], 'w_mem_q': w_mem_q[l],
             'w_mem_kv': w_mem_kv[l], 'w_mem_o': w_mem_o[l], 'norm_ffn': norm_ffn[l]}
        i = l // 2
        if l % 2 == 0:
            ffn = lambda h, i=i: swiglu(h, ffn_gu[i], ffn_down[i])
        else:
            ffn = lambda h, i=i: moe_ffn(h, moe_router[i], moe_router_b[i], moe_gu[i], moe_down[i])
        lam0 = lambda_init(l)
        mk_p, mv_p = memory_kv(mem_prompt, p)
        empty_kv = jnp.zeros((B, 0, N_ATT_HEADS, HEAD_DIM), dt)
        xp, kp, vp, lfp, wkvp, shp = run_layer(
            xp, empty_kv, empty_kv, jnp.zeros((B, 0, G), dt),
            jnp.zeros((B, G, HEAD_DIM, HEAD_DIM), dt), jnp.zeros((B, RWKV_W), dt),
            mk_p, mv_p, p, lam0, ffn)
        pk = cache_k[l, page_table].reshape(DB, past_len, N_ATT_HEADS, HEAD_DIM)
        pv = cache_v[l, page_table].reshape(DB, past_len, N_ATT_HEADS, HEAD_DIM)
        plf = cache_logf[l, page_table].reshape(DB, past_len, G)
        xs, ks_, vs_, lfs, wkvs, shs = run_layer(
            xs, pk, pv, plf, state_wkv[l], state_shift[l], cache_mem_k[l], cache_mem_v[l],
            p, lam0, ffn)
        kp_l.append(kp); vp_l.append(vp); lfp_l.append(lfp); mkp_l.append(mk_p); mvp_l.append(mv_p)
        wkvp_l.append(wkvp); shp_l.append(shp)
        ks_l.append(ks_); vs_l.append(vs_); lfs_l.append(lfs); wkvs_l.append(wkvs); shs_l.append(shs)
    y_prompt = rmsnorm(xp, norm_final)
    y_sample = rmsnorm(xs, norm_final)
    new_k_prompt = jnp.stack(kp_l)
    new_v_prompt = jnp.stack(vp_l)
    new_logf_prompt = jnp.stack(lfp_l)
    new_mem_k_prompt = jnp.stack(mkp_l)
    new_mem_v_prompt = jnp.stack(mvp_l)
    new_wkv_prompt = jnp.stack(wkvp_l)
    new_shift_prompt = jnp.stack(shp_l)
    new_k_sample = jnp.stack(ks_l)
    new_v_sample = jnp.stack(vs_l)
    new_logf_sample = jnp.stack(lfs_l)
    new_wkv_sample = jnp.stack(wkvs_l)
    new_shift_sample = jnp.stack(shs_l)
    return (y_prompt, y_sample, new_k_prompt, new_v_prompt, new_logf_prompt, new_mem_k_prompt,
            new_mem_v_prompt, new_wkv_prompt, new_shift_prompt, new_k_sample, new_v_sample,
            new_logf_sample, new_wkv_sample, new_shift_sample)
```

```python
import functools
import math

import jax
import jax.numpy as jnp
from jax import lax
from jax.experimental import pallas as pl
from jax.experimental.pallas import tpu as pltpu

f32 = jnp.float32
bf16 = jnp.bfloat16

D_MODEL = 1024
HEAD_DIM = 64
G_HEADS = 4
GROUP_W = G_HEADS * HEAD_DIM
ATT_W = 3 * GROUP_W
RWKV_W = 1024
IN_W = 3 * ATT_W + G_HEADS + RWKV_W
PROJ_W = RWKV_W + 3 * ATT_W + 128
PR_COL = 0
Q_COL = RWKV_W
K_COL = Q_COL + ATT_W
V_COL = K_COL + ATT_W
FL_COL = V_COL + ATT_W
MOBA_BLOCK = 256
MOBA_TOPK = 3
PAGE = 128
N_MEM = 256
D_FF = 2816
N_EXPERTS = 8
NORM_EPS = 1e-6
GN_EPS = 64e-5
NEG = -1e30
LANES = 128
RW_CHUNK = 64
VMEM_LIMIT = 56 * 1024 * 1024

HI = lax.Precision.HIGHEST


def _cparams(*sem):
    return pltpu.CompilerParams(dimension_semantics=sem, vmem_limit_bytes=VMEM_LIMIT)


def _dot(a, b, prec=None):
    return jnp.dot(a, b, preferred_element_type=f32, precision=prec)


def _dot_nt(a, b, prec=None):
    return lax.dot_general(a, b, (((1,), (1,)), ((), ())), preferred_element_type=f32, precision=prec)


def _dot_tn(a, b, prec=None):
    return lax.dot_general(a, b, (((0,), (0,)), ((), ())), preferred_element_type=f32, precision=prec)


def _rms(x, g):
    return x * lax.rsqrt(jnp.mean(x * x, axis=-1, keepdims=True) + NORM_EPS) * g


def _pair_rmsnorm(o, g_pair):
    lane = lax.broadcasted_iota(jnp.int32, o.shape, 1)
    lo = lane < HEAD_DIM
    sq = o * o
    s_lo = jnp.sum(jnp.where(lo, sq, 0.0), axis=-1, keepdims=True)
    s_hi = jnp.sum(jnp.where(lo, 0.0, sq), axis=-1, keepdims=True)
    ms = jnp.where(lo, s_lo, s_hi) * (1.0 / HEAD_DIM)
    return o * lax.rsqrt(ms + NORM_EPS) * g_pair


def _norm_matmul_kernel(x_ref, g_ref, w_ref, b_ref, o_ref, *, chunk, logsig_tail):
    h = _rms(x_ref[...], g_ref[...]).astype(bf16)
    n_out = o_ref.shape[1]
    for c0 in range(0, n_out, chunk):
        c1 = min(c0 + chunk, n_out)
        y = _dot(h, w_ref[:, c0:c1])
        if logsig_tail and c1 == n_out:
            t0 = c1 - c0 - LANES
            tail = jax.nn.log_sigmoid(y[:, t0:] + b_ref[...])
            o_ref[:, c0:c0 + t0] = y[:, :t0]
            o_ref[:, c0 + t0:c1] = tail
        else:
            o_ref[:, c0:c1] = y


def _norm_matmul(x, g, w, bias_pad, *, tm, chunk, logsig_tail):
    n, d = x.shape
    n_out = w.shape[1]
    return pl.pallas_call(
        functools.partial(_norm_matmul_kernel, chunk=chunk, logsig_tail=logsig_tail),
        grid=(n // tm,),
        in_specs=[pl.BlockSpec((tm, d), lambda i: (i, 0)),
                  pl.BlockSpec((1, d), lambda i: (0, 0)),
                  pl.BlockSpec((d, n_out), lambda i: (0, 0)),
                  pl.BlockSpec((1, LANES), lambda i: (0, 0))],
        out_specs=pl.BlockSpec((tm, n_out), lambda i: (i, 0)),
        out_shape=jax.ShapeDtypeStruct((n, n_out), f32),
        compiler_params=_cparams("parallel"),
        name="norm_matmul",
    )(x, g, w, bias_pad)


def _cumsum_kernel(lf_ref, o_ref, carry_ref, *, tc):
    @pl.when(pl.program_id(1) == 0)
    def _():
        carry_ref[...] = jnp.zeros_like(carry_ref)
    r = lax.broadcasted_iota(jnp.int32, (tc, tc), 0)
    c = lax.broadcasted_iota(jnp.int32, (tc, tc), 1)
    tri = jnp.where(c <= r, 1.0, 0.0).astype(f32)
    cum = _dot(tri, lf_ref[...], HI) + carry_ref[...]
    o_ref[...] = cum
    carry_ref[...] = cum[tc - 1:tc, :]


def _cumsum_logf(proj, n_b, t):
    tc = 256
    nt = t // tc
    return pl.pallas_call(
        functools.partial(_cumsum_kernel, tc=tc),
        grid=(n_b, nt),
        in_specs=[pl.BlockSpec((tc, LANES), lambda b, i: (b * nt + i, FL_COL // LANES))],
        out_specs=pl.BlockSpec((tc, LANES), lambda b, i: (b * nt + i, 0)),
        out_shape=jax.ShapeDtypeStruct((n_b * t, LANES), f32),
        scratch_shapes=[pltpu.VMEM((1, LANES), f32)],
        compiler_params=_cparams("parallel", "arbitrary"),
        name="cumsum_logf",
    )(proj)


def _attn_kernel(*refs, mode, tq, lam_scale):
    if mode == "diff":
        q_ref, k_ref, v_ref, par_ref, o_ref, m_s, l_s, acc_s = refs
    elif mode == "fox":
        q_ref, k_ref, v_ref, ck_ref, o_ref, m_s, l_s, acc_s = refs
    else:
        q_ref, k_ref, v_ref, o_ref, m_s, l_s, acc_s, kmean_s, sel_s = refs
    i = pl.program_id(2)
    nvar = 4 if mode == "diff" else 2
    rows = nvar * tq
    tk = tq
    lane = lax.broadcasted_iota(jnp.int32, (tq, LANES), 1)
    q = q_ref[...]
    if mode == "diff":
        half = HEAD_DIM // 2
        scale = half ** -0.5
        parts = [jnp.where((lane >= v * half) & (lane < (v + 1) * half), q, 0.0) for v in range(4)]
    else:
        scale = HEAD_DIM ** -0.5
        parts = [jnp.where(lane < HEAD_DIM, q, 0.0), jnp.where(lane >= HEAD_DIM, q, 0.0)]
    qs32 = jnp.concatenate(parts, axis=0) * scale
    qs = qs32.astype(bf16)

    m_s[...] = jnp.full(m_s.shape, NEG, f32)
    l_s[...] = jnp.zeros(l_s.shape, f32)
    acc_s[...] = jnp.zeros(acc_s.shape, f32)

    row = lax.broadcasted_iota(jnp.int32, (rows, tk), 0)
    col = lax.broadcasted_iota(jnp.int32, (rows, tk), 1)
    causal = col <= (row & (tq - 1))

    if mode == "moba":
        nb = k_ref.shape[0] // MOBA_BLOCK

        @pl.when(i == 0)
        def _():
            kmean_s[...] = jnp.zeros(kmean_s.shape, f32)

            def blk(n, _):
                kb = k_ref[pl.ds(pl.multiple_of(n * MOBA_BLOCK, MOBA_BLOCK), MOBA_BLOCK), :]
                kmean_s[pl.ds(n, 1), :] = jnp.mean(kb, axis=0, keepdims=True)
                return 0
            lax.fori_loop(0, nb, blk, 0)

        gate = _dot_nt(qs32, kmean_s[...], HI)
        nidx = lax.broadcasted_iota(jnp.int32, (rows, LANES), 1)
        nidx_f = nidx.astype(f32)
        r1 = lax.broadcasted_iota(jnp.int32, (rows, 1), 0)
        own = (i * tq + (r1 & (tq - 1))) // MOBA_BLOCK
        g = jnp.where(nidx < own, gate, NEG)
        sel = jnp.zeros((rows, LANES), f32)
        for _ in range(MOBA_TOPK):
            mx = jnp.max(g, axis=-1, keepdims=True)
            first = jnp.min(jnp.where(g == mx, nidx_f, 1e9), axis=-1, keepdims=True)
            pick = nidx_f == first
            ok = first < own.astype(f32)
            sel = jnp.where(pick & ok, 1.0, sel)
            g = jnp.where(pick, -3e38, g)
        sel_s[...] = sel

    def step(j, diagonal):
        start = pl.multiple_of(j * tk, tk)
        kb = k_ref[pl.ds(start, tk), :].astype(bf16)
        vb = v_ref[pl.ds(start, tk), :].astype(bf16)
        s = _dot_nt(qs, kb)
        if mode == "fox":
            ckb = ck_ref[0, :, pl.ds(start, tk)]
            r2 = lax.broadcasted_iota(jnp.int32, (rows, 1), 0)
            s = s - jnp.where(r2 < tq, ckb[0:1, :], ckb[1:2, :])
        valid = None
        if mode == "moba" and not diagonal:
            nidx2 = lax.broadcasted_iota(jnp.int32, (rows, LANES), 1)
            selj = jnp.sum(jnp.where(nidx2 == j, sel_s[...], 0.0), axis=-1, keepdims=True) > 0.5
            valid = selj
        if diagonal:
            valid = causal
        if valid is not None:
            s = jnp.where(valid, s, NEG)
        m_old = m_s[...]
        m_new = jnp.maximum(m_old, jnp.max(s, axis=-1, keepdims=True))
        alpha = jnp.exp(m_old - m_new)
        p = jnp.exp(s - m_new)
        if valid is not None:
            p = jnp.where(valid, p, 0.0)
        l_s[...] = alpha * l_s[...] + jnp.sum(p, axis=-1, keepdims=True)
        acc_s[...] = alpha * acc_s[...] + _dot(p.astype(bf16), vb)
        m_s[...] = m_new

    def body(j, _):
        step(j, False)
        return 0
    lax.fori_loop(0, i, body, 0)
    step(i, True)

    inv_l = 1.0 / l_s[...]
    a = acc_s[...] * inv_l
    lo = lane < HEAD_DIM
    if mode == "diff":
        lam = par_ref[0:1, :]
        o_h0 = a[0:tq] - lam * a[tq:2 * tq]
        o_h1 = a[2 * tq:3 * tq] - lam * a[3 * tq:4 * tq]
        o = jnp.where(lo, o_h0, o_h1)
        o_ref[...] = _pair_rmsnorm(o, par_ref[1:2, :]) * lam_scale
    else:
        o_ref[...] = jnp.where(lo, a[0:tq], a[tq:2 * tq])


def _attn_prompt(proj, n_b, t, mode, group, extra=None, lam_scale=1.0):
    tq = 256
    nq = t // tq
    nvar = 4 if mode == "diff" else 2
    rows = nvar * tq
    qc, kc, vc = (c // LANES + 2 * group for c in (Q_COL, K_COL, V_COL))
    in_specs = [pl.BlockSpec((tq, LANES), lambda b, hp, i: (b * nq + i, qc + hp)),
                pl.BlockSpec((t, LANES), lambda b, hp, i: (b, kc + hp)),
                pl.BlockSpec((t, LANES), lambda b, hp, i: (b, vc + hp))]
    args = [proj, proj, proj]
    scratch = [pltpu.VMEM((rows, 1), f32), pltpu.VMEM((rows, 1), f32), pltpu.VMEM((rows, LANES), f32)]
    if mode == "diff":
        in_specs.append(pl.BlockSpec((8, LANES), lambda b, hp, i: (0, 0)))
        args.append(extra)
    elif mode == "fox":
        in_specs.append(pl.BlockSpec((1, 2, t), lambda b, hp, i: (b * 2 + hp, 0, 0)))
        args.append(extra)
    else:
        scratch += [pltpu.VMEM((LANES, LANES), f32), pltpu.VMEM((rows, LANES), f32)]
    return pl.pallas_call(
        functools.partial(_attn_kernel, mode=mode, tq=tq, lam_scale=lam_scale),
        grid=(n_b, 2, nq),
        in_specs=in_specs,
        out_specs=pl.BlockSpec((tq, LANES), lambda b, hp, i: (b * nq + i, hp)),
        out_shape=jax.ShapeDtypeStruct((n_b * t, GROUP_W), f32),
        scratch_shapes=scratch,
        compiler_params=_cparams("parallel", "parallel", "arbitrary"),
        name="attn_" + mode,
    )(*args)


def _head_ones(n):
    r = lax.broadcasted_iota(jnp.int32, (n, n), 0) // HEAD_DIM
    c = lax.broadcasted_iota(jnp.int32, (n, n), 1) // HEAD_DIM
    return jnp.where(r == c, 1.0, 0.0).astype(f32)


def _rw_prep_kernel(pr_ref, prev_ref, mu_ref, vec_ref, w2_ref, a2_ref, g2_ref, o_ref):
    pr = pr_ref[...]
    xm = pr + (prev_ref[...] - pr) * mu_ref[...]
    r = xm[:, 0:GROUP_W]
    k = xm[:, GROUP_W:2 * GROUP_W]
    v = xm[:, 2 * GROUP_W:3 * GROUP_W]
    x_lora = xm[:, 3 * GROUP_W:3 * GROUP_W + LANES]
    x_gate = xm[:, 3 * GROUP_W + LANES:]
    w0, a0, kk_p, ka, rk = (vec_ref[n:n + 1, :] for n in range(5))
    w_lin = w0 + _dot(jnp.tanh(x_lora).astype(bf16), w2_ref[...])
    w_log = -jax.nn.softplus(-w_lin) - 0.5
    lw = -jnp.exp(w_log)
    a = jax.nn.sigmoid(a0 + _dot(x_lora.astype(bf16), a2_ref[...]))
    g = _dot(jax.nn.sigmoid(x_gate).astype(bf16), g2_ref[...])
    ones = _head_ones(GROUP_W)
    kk = k * kk_p
    kk = kk * lax.rsqrt(_dot(kk * kk, ones, HI) + 1e-12)
    kh = k * (1.0 + (a - 1.0) * ka)
    bonus = _dot(r * kh * rk, ones, HI) * v
    for n, val in enumerate((r, lw, kh, v, kk, kk * a, g, bonus)):
        o_ref[:, n * GROUP_W:(n + 1) * GROUP_W] = val


def _rw_prep(proj, prev, mu, vecs, w2p, a2p, g2, tm):
    n = proj.shape[0]
    full = lambda shape: pl.BlockSpec(shape, lambda i: (0,) * len(shape))
    return pl.pallas_call(
        _rw_prep_kernel,
        grid=(n // tm,),
        in_specs=[pl.BlockSpec((tm, RWKV_W), lambda i: (i, PR_COL // RWKV_W)),
                  pl.BlockSpec((tm, RWKV_W), lambda i: (i, 0)),
                  full((1, RWKV_W)), full((8, GROUP_W)), full((LANES, GROUP_W)), full((LANES, GROUP_W)),
                  full((LANES, GROUP_W))],
        out_specs=pl.BlockSpec((tm, 8 * GROUP_W), lambda i: (i, 0)),
        out_shape=jax.ShapeDtypeStruct((n, 8 * GROUP_W), f32),
        compiler_params=_cparams("parallel"),
        name="rw_prep",
    )(proj, prev, mu, vecs, w2p, a2p, g2)


def _rw_chunk_kernel(x_ref, m0_ref, ln_ref, o_ref, mfin_ref, m_s):
    c = pl.program_id(1)
    nc = pl.num_programs(1)
    C = RW_CHUNK
    W = GROUP_W

    @pl.when(c == 0)
    def _():
        m_s[...] = m0_ref[...]

    r, lw, kh, v, kk, b, g, bonus = (x_ref[:, n * W:(n + 1) * W] for n in range(8))

    tr = lax.broadcasted_iota(jnp.int32, (C, C), 0)
    tc = lax.broadcasted_iota(jnp.int32, (C, C), 1)
    cum = _dot(jnp.where(tc <= tr, 1.0, 0.0).astype(f32), lw, HI)
    g_inc = jnp.exp(cum)
    inv = jnp.exp(-cum)
    g_end = jnp.exp(cum[C - 1:C, :])

    ri = lax.broadcasted_iota(jnp.int32, (W, W), 0)
    ci = lax.broadcasted_iota(jnp.int32, (W, W), 1)
    same = (ri // C) == (ci // HEAD_DIM)
    same_t = (ri // C) == (ci // C)
    strict = same_t & ((ci % C) < (ri % C))
    incl = same_t & ((ci % C) <= (ri % C))

    def sm(x):
        return jnp.where(same, jnp.concatenate([x] * G_HEADS, axis=0), 0.0)

    a_sm = sm(-kk * jnp.exp(cum - lw))
    r_sm = sm(r * g_inc)
    b_sm = sm(b * inv)
    k_sm = sm(kh * inv)
    v_sm = sm(v)

    def mm(x, y):
        return _dot(x.astype(bf16), y.astype(bf16))

    gram = _dot_nt(jnp.concatenate([a_sm, r_sm], axis=0).astype(bf16),
                   jnp.concatenate([b_sm, k_sm], axis=0).astype(bf16))
    n_ab = jnp.where(strict, gram[:W, :W], 0.0)
    a_ak = jnp.where(strict, gram[:W, W:], 0.0)
    a_rb = jnp.where(incl, gram[W:, :W], 0.0)
    a_rk = jnp.where(incl, gram[W:, W:], 0.0)

    eye = jnp.where(ri == ci, 1.0, 0.0).astype(f32)
    t_inv = eye + n_ab
    n_pow = n_ab
    for _ in range(int(math.log2(C)) - 1):
        n_pow = mm(n_pow, n_pow)
        t_inv = t_inv + mm(n_pow, t_inv)

    akv = mm(a_ak, v_sm)
    tx = mm(t_inv, jnp.concatenate([a_sm, akv], axis=1))
    y = mm(a_rb, tx)
    w_o = r_sm + y[:, :W]
    o_0 = y[:, W:] + mm(a_rk, v_sm)
    bt = (b_sm * g_end).astype(bf16)
    kt = (k_sm * g_end).astype(bf16)
    pq = _dot_tn(bt, tx.astype(bf16))
    p_mat = jnp.where(ri == ci, g_end, 0.0) + pq[:, :W]
    q_mat = pq[:, W:] + _dot_tn(kt, v_sm.astype(bf16))

    m_prev = m_s[...]
    o_sm = _dot(w_o, m_prev, HI) + o_0
    m_new = _dot(p_mat, m_prev, HI) + q_mat
    m_s[...] = m_new
    o = o_sm[0:C] + o_sm[C:2 * C] + o_sm[2 * C:3 * C] + o_sm[3 * C:4 * C]

    mean_mat = _head_ones(W) * (1.0 / HEAD_DIM)
    mu = _dot(o, mean_mat, HI)
    d = o - mu
    var = _dot(d * d, mean_mat, HI)
    o_n = d * lax.rsqrt(var + GN_EPS) * ln_ref[0:1, :] + ln_ref[1:2, :]
    o_ref[...] = (o_n + bonus) * g

    @pl.when(c == nc - 1)
    def _():
        mfin_ref[...] = m_new


def _rw_chunk(rw, m0, ln, n_b, t):
    nc = t // RW_CHUNK
    return pl.pallas_call(
        _rw_chunk_kernel,
        grid=(n_b, nc),
        in_specs=[pl.BlockSpec((RW_CHUNK, 8 * GROUP_W), lambda b, c: (b * nc + c, 0)),
                  pl.BlockSpec((None, GROUP_W, GROUP_W), lambda b, c: (b, 0, 0)),
                  pl.BlockSpec((8, GROUP_W), lambda b, c: (0, 0))],
        out_specs=[pl.BlockSpec((RW_CHUNK, GROUP_W), lambda b, c: (b * nc + c, 0)),
                   pl.BlockSpec((None, GROUP_W, GROUP_W), lambda b, c: (b, 0, 0))],
        out_shape=[jax.ShapeDtypeStruct((n_b * t, GROUP_W), f32),
                   jax.ShapeDtypeStruct((n_b, GROUP_W, GROUP_W), f32)],
        scratch_shapes=[pltpu.VMEM((GROUP_W, GROUP_W), f32)],
        compiler_params=_cparams("parallel", "arbitrary"),
        name="rw_chunk",
    )(rw, m0, ln)


N_COLS = 64
COL_B, COL_C = 32, 48
PAGES_PER_STEP = 2


def _col_maps(shape, axis):
    c = lax.broadcasted_iota(jnp.int32, shape, axis)
    tok = c % 4
    head = jnp.where(c < COL_B, c // 8, jnp.where(c < COL_C, 4 + (c - COL_B) // 4, 8 + (c - COL_C) // 4))
    half = (c // 4) % 2
    lo = head * HEAD_DIM + jnp.where(c < COL_B, half * (HEAD_DIM // 2), 0)
    hi = jnp.where(c < COL_B, lo + HEAD_DIM // 2, lo + HEAD_DIM)
    return c, tok, head, lo, hi


def _decode_kernel(pt_ref, q_ref, kn_ref, vn_ref, ln_ref, par_ref, k0_ref, k1_ref, v0_ref, v1_ref, l0_ref, l1_ref,
                   o_ref, qbd_s, m_s, l_s, acc_s, carry_s, bm_s, bl_s, bg_s, bacc_s, *, lam_scale, nj):
    j = pl.program_id(1)
    kw = PAGES_PER_STEP * PAGE
    c1, tok1, head1, lo1, hi1 = _col_maps((N_COLS, 1), 0)

    @pl.when(j == 0)
    def _():
        q4 = q_ref[...]
        lanes = lax.broadcasted_iota(jnp.int32, (N_COLS, ATT_W), 1)
        qb = jnp.zeros((N_COLS, ATT_W), f32)
        for t in range(4):
            qb = jnp.where(tok1 == t, q4[t:t + 1, :], qb)
        scale = jnp.where(c1 < COL_B, (HEAD_DIM // 2) ** -0.5, HEAD_DIM ** -0.5)
        qbd_s[...] = (jnp.where((lanes >= lo1) & (lanes < hi1), qb, 0.0) * scale).astype(bf16)
        m_s[...] = jnp.full(m_s.shape, NEG, f32)
        l_s[...] = jnp.zeros(l_s.shape, f32)
        acc_s[...] = jnp.zeros(acc_s.shape, f32)
        carry_s[...] = jnp.zeros(carry_s.shape, f32)
        bm_s[...] = jnp.zeros(bm_s.shape, f32)
        bl_s[...] = jnp.zeros(bl_s.shape, f32)
        bg_s[...] = jnp.zeros(bg_s.shape, f32)

    qbd = qbd_s[...]
    is_c = c1 >= COL_C

    def forget_bias(cum):
        bias = jnp.zeros((N_COLS, cum.shape[1]), f32)
        for h in range(G_HEADS):
            bias = jnp.where(is_c & (head1 == 8 + h), -cum[h:h + 1, :], bias)
        return bias

    def upper(n):
        r = lax.broadcasted_iota(jnp.int32, (n, n), 0)
        c = lax.broadcasted_iota(jnp.int32, (n, n), 1)
        return jnp.where(r <= c, 1.0, 0.0).astype(f32)

    def merge(m_blk, l_blk, acc_blk):
        m_old = m_s[...]
        m_new = jnp.maximum(m_old, m_blk)
        a = jnp.exp(m_old - m_new)
        bb = jnp.exp(m_blk - m_new)
        l_s[...] = a * l_s[...] + bb * l_blk
        acc_s[...] = a * acc_s[...] + bb * acc_blk
        m_s[...] = m_new

    k0 = k0_ref[...].astype(bf16)
    k1 = k1_ref[...].astype(bf16)
    s = jnp.concatenate([_dot(qbd, k0), _dot(qbd, k1)], axis=1)
    lf = jnp.concatenate([l0_ref[...], l1_ref[...]], axis=1)
    cum = _dot(lf, upper(kw), HI) + carry_s[0:G_HEADS, 0:1]
    carry_s[0:G_HEADS, :] = jnp.broadcast_to(cum[:, kw - 1:kw], (G_HEADS, LANES))
    gate = jnp.sum(s[COL_B:COL_C], axis=-1, keepdims=True) * (1.0 / MOBA_BLOCK)
    s = s + forget_bias(cum)
    m_blk = jnp.max(s, axis=-1, keepdims=True)
    p = jnp.exp(s - m_blk)
    l_blk = jnp.sum(p, axis=-1, keepdims=True)
    pb = p.astype(bf16)
    acc_blk = _dot_nt(pb[:, :PAGE], v0_ref[...].astype(bf16)) + _dot_nt(pb[:, PAGE:], v1_ref[...].astype(bf16))
    merge(m_blk, l_blk, acc_blk)
    blane = lax.broadcasted_iota(jnp.int32, (COL_C - COL_B, LANES), 1)
    bm_s[...] = jnp.where(blane == j, m_blk[COL_B:COL_C], bm_s[...])
    bl_s[...] = jnp.where(blane == j, l_blk[COL_B:COL_C], bl_s[...])
    bg_s[...] = jnp.where(blane == j, gate, bg_s[...])
    bacc_s[j] = acc_blk[COL_B:COL_C]

    @pl.when(j == nj - 1)
    def _():
        sn = _dot(qbd, kn_ref[...].astype(bf16))
        cum_n = _dot(ln_ref[...], upper(PAGE), HI) + carry_s[0:G_HEADS, 0:1]
        klane = lax.broadcasted_iota(jnp.int32, (N_COLS, PAGE), 1)
        sn = jnp.where(klane <= tok1, sn + forget_bias(cum_n), NEG)
        m_n = jnp.max(sn, axis=-1, keepdims=True)
        p_n = jnp.exp(sn - m_n)
        l_n = jnp.sum(p_n, axis=-1, keepdims=True)
        acc_n = _dot_nt(p_n.astype(bf16), vn_ref[...].astype(bf16))
        merge(m_n, l_n, acc_n)

        nb = nj
        bgate = jnp.where(blane < nb, bg_s[...], NEG)
        blane_f = blane.astype(f32)
        sel = jnp.zeros(bgate.shape, f32)
        for _ in range(MOBA_TOPK):
            mx = jnp.max(bgate, axis=-1, keepdims=True)
            first = jnp.min(jnp.where(bgate == mx, blane_f, 1e9), axis=-1, keepdims=True)
            pick = blane_f == first
            sel = jnp.where(pick & (first < nb), 1.0, sel)
            bgate = jnp.where(pick, -3e38, bgate)
        on = sel > 0.5
        mb_n, lb_n, accb_n = m_n[COL_B:COL_C], l_n[COL_B:COL_C], acc_n[COL_B:COL_C]
        m_tot = jnp.maximum(jnp.max(jnp.where(on, bm_s[...], NEG), axis=-1, keepdims=True), mb_n)
        wgt = jnp.where(on, jnp.exp(bm_s[...] - m_tot), 0.0)
        e_n = jnp.exp(mb_n - m_tot)
        l_b = jnp.sum(wgt * bl_s[...], axis=-1, keepdims=True) + e_n * lb_n

        def add_block(n, acc):
            w_n = jnp.sum(jnp.where(blane == n, wgt, 0.0), axis=-1, keepdims=True)
            return acc + w_n * bacc_s[n]
        acc_b = lax.fori_loop(0, nb, add_block, e_n * accb_n)
        l_s[COL_B:COL_C, :] = l_b
        acc_s[COL_B:COL_C, :] = acc_b

        lam = par_ref[0:1, 0:1]
        coef = jnp.where((c1 < COL_B) & ((c1 // 4) % 2 == 1), -lam, 1.0) / l_s[...]
        lanes = lax.broadcasted_iota(jnp.int32, (N_COLS, ATT_W), 1)
        keep = (lanes >= head1 * HEAD_DIM) & (lanes < (head1 + 1) * HEAD_DIM)
        contrib = jnp.where(keep, acc_s[...] * coef, 0.0)
        _, tok_r, _, _, _ = _col_maps((8, N_COLS), 1)
        trow = lax.broadcasted_iota(jnp.int32, (8, N_COLS), 0)
        o_tok = _dot(jnp.where(tok_r == trow, 1.0, 0.0).astype(f32), contrib, HI)[0:4]
        o_ref[:, 2 * LANES:] = o_tok[:, 2 * LANES:]
        for hp in range(2):
            o_ref[:, hp * LANES:(hp + 1) * LANES] = (
                _pair_rmsnorm(o_tok[:, hp * LANES:(hp + 1) * LANES], par_ref[1:2, :]) * lam_scale)


def _decode_attn(page_table, q_s, kn_t, vn_t, ln_t, par, k_t, v_t, lf_t, layer, lam_scale):
    n_b, n_pages = page_table.shape
    nj = n_pages // PAGES_PER_STEP

    def page_spec(rows, g):
        return pl.BlockSpec((None, None, rows, PAGE), lambda b, j, pt: (layer, pt[b, PAGES_PER_STEP * j + g], 0, 0))
    new_spec = lambda rows: pl.BlockSpec((None, rows, PAGE), lambda b, j, pt: (b, 0, 0))
    nbc = COL_C - COL_B
    return pl.pallas_call(
        functools.partial(_decode_kernel, lam_scale=lam_scale, nj=nj),
        grid_spec=pltpu.PrefetchScalarGridSpec(
            num_scalar_prefetch=1,
            grid=(n_b, nj),
            in_specs=[pl.BlockSpec((None, 4, ATT_W), lambda b, j, pt: (b, 0, 0)),
                      new_spec(ATT_W), new_spec(ATT_W), new_spec(G_HEADS),
                      pl.BlockSpec((8, LANES), lambda b, j, pt: (0, 0)),
                      page_spec(ATT_W, 0), page_spec(ATT_W, 1), page_spec(ATT_W, 0), page_spec(ATT_W, 1),
                      page_spec(G_HEADS, 0), page_spec(G_HEADS, 1)],
            out_specs=pl.BlockSpec((None, 4, ATT_W), lambda b, j, pt: (b, 0, 0)),
            scratch_shapes=[pltpu.VMEM((N_COLS, ATT_W), bf16),
                            pltpu.VMEM((N_COLS, 1), f32), pltpu.VMEM((N_COLS, 1), f32),
                            pltpu.VMEM((N_COLS, ATT_W), f32), pltpu.VMEM((8, LANES), f32),
                            pltpu.VMEM((nbc, LANES), f32), pltpu.VMEM((nbc, LANES), f32),
                            pltpu.VMEM((nbc, LANES), f32), pltpu.VMEM((nj, nbc, ATT_W), f32)]),
        out_shape=jax.ShapeDtypeStruct((n_b, 4, ATT_W), f32),
        compiler_params=_cparams("parallel", "arbitrary"),
        name="decode_attn",
    )(page_table, q_s, kn_t, vn_t, ln_t, par, k_t, k_t, v_t, v_t, lf_t, lf_t)


def _post_mixer_kernel(*refs, widths):
    n_o = len(widths)
    x_ref = refs[0]
    o_refs = refs[1:1 + n_o]
    w_out_ref, g_ref, wq_ref, mk_ref, mv_ref, wo_ref, y_ref = refs[1 + n_o:]
    x = x_ref[...]
    r0 = 0
    for o_r, w in zip(o_refs, widths):
        x = x + _dot(o_r[...].astype(bf16), w_out_ref[r0:r0 + w, :])
        r0 += w
    h = _rms(x, g_ref[...]).astype(bf16)
    q = _dot(h, wq_ref[...]) * (HEAD_DIM ** -0.5)
    lane = lax.broadcasted_iota(jnp.int32, q.shape, 1)
    mk = mk_ref[...].astype(bf16)
    mv = mv_ref[...].astype(bf16)
    o2 = jnp.zeros(q.shape, f32)
    for hd in range(G_HEADS):
        in_head = (lane >= hd * HEAD_DIM) & (lane < (hd + 1) * HEAD_DIM)
        s = _dot(jnp.where(in_head, q, 0.0).astype(bf16), mk)
        p = jnp.exp(s - jnp.max(s, axis=-1, keepdims=True))
        p = p / jnp.sum(p, axis=-1, keepdims=True)
        o2 = jnp.where(in_head, _dot_nt(p.astype(bf16), mv), o2)
    y_ref[...] = x + _dot(o2.astype(bf16), wo_ref[...])


def _post_mixer(x3, o_list, w_out, g, wq, mk_t, mv_t, wo, tm):
    n_b, t, d = x3.shape
    widths = tuple(o.shape[-1] for o in o_list)
    full = lambda a: pl.BlockSpec(a.shape, lambda b, i: (0,) * a.ndim)
    tok = lambda w: pl.BlockSpec((None, tm, w), lambda b, i: (b, i, 0))
    mem = pl.BlockSpec((None, GROUP_W, N_MEM), lambda b, i: (b, 0, 0))
    return pl.pallas_call(
        functools.partial(_post_mixer_kernel, widths=widths),
        grid=(n_b, t // tm),
        in_specs=[tok(d)] + [tok(w) for w in widths] + [full(w_out), full(g), full(wq), mem, mem, full(wo)],
        out_specs=tok(d),
        out_shape=jax.ShapeDtypeStruct(x3.shape, f32),
        compiler_params=_cparams("parallel", "parallel"),
        name="post_mixer",
    )(x3, *o_list, w_out, g, wq, mk_t, mv_t, wo)


def _ffn_kernel(*refs, moe, final_norm):
    if moe:
        x_ref, g_ref, rt_ref, rb_ref, wg_ref, wu_ref, wd_ref, gf_ref, y_ref, h_s, acc_s, tmp_s, gate_s = refs
    else:
        x_ref, g_ref, wg_ref, wu_ref, wd_ref, gf_ref, y_ref, h_s, acc_s = refs
    e = pl.program_id(1)
    f = pl.program_id(2)
    ne = pl.num_programs(1)
    nf = pl.num_programs(2)

    @pl.when((e == 0) & (f == 0))
    def _():
        h = _rms(x_ref[...], g_ref[...])
        h_s[...] = h.astype(bf16)
        acc_s[...] = jnp.zeros(acc_s.shape, f32)
        if moe:
            logits = _dot(h, rt_ref[...], HI) + rb_ref[...]
            lane = lax.broadcasted_iota(jnp.int32, logits.shape, 1)
            lane_f = lane.astype(f32)
            lg = jnp.where(lane < N_EXPERTS, logits, -3e38)
            v1 = jnp.max(lg, axis=-1, keepdims=True)
            i1 = jnp.min(jnp.where(lg == v1, lane_f, 1e9), axis=-1, keepdims=True)
            lg2 = jnp.where(lane_f == i1, -3e38, lg)
            v2 = jnp.max(lg2, axis=-1, keepdims=True)
            i2 = jnp.min(jnp.where(lg2 == v2, lane_f, 1e9), axis=-1, keepdims=True)
            e2 = jnp.exp(v2 - v1)
            den = 1.0 + e2
            gate_s[...] = jnp.where(lane_f == i1, 1.0 / den, jnp.where(lane_f == i2, e2 / den, 0.0))

    h = h_s[...]
    gg = _dot(h, wg_ref[...])
    uu = _dot(h, wu_ref[...])
    act = (gg * jax.nn.sigmoid(gg) * uu).astype(bf16)
    part = _dot(act, wd_ref[...])
    if moe:
        @pl.when(f == 0)
        def _():
            tmp_s[...] = part

        @pl.when(f > 0)
        def _():
            tmp_s[...] = tmp_s[...] + part

        @pl.when(f == nf - 1)
        def _():
            lane = lax.broadcasted_iota(jnp.int32, gate_s.shape, 1)
            ge = jnp.sum(jnp.where(lane == e, gate_s[...], 0.0), axis=-1, keepdims=True)
            acc_s[...] = acc_s[...] + ge * tmp_s[...]
    else:
        acc_s[...] = acc_s[...] + part

    @pl.when((e == ne - 1) & (f == nf - 1))
    def _():
        y = x_ref[...] + acc_s[...]
        if final_norm:
            y = _rms(y, gf_ref[...])
        y_ref[...] = y


def _ffn(x, g, w_gu, w_d, g_final, *, tm, tf, router=None, final_norm=False):
    n, d = x.shape
    moe = router is not None
    ne = w_gu.shape[0]
    nf = D_FF // tf
    full = lambda a: pl.BlockSpec(a.shape, lambda i, e, f: (0,) * a.ndim)
    in_specs = [pl.BlockSpec((tm, d), lambda i, e, f: (i, 0)), full(g)]
    args = [x, g]
    if moe:
        in_specs += [full(router[0]), full(router[1])]
        args += list(router)
    in_specs += [pl.BlockSpec((None, d, tf), lambda i, e, f: (e, 0, f)),
                 pl.BlockSpec((None, d, tf), lambda i, e, f: (e, 0, nf + f)),
                 pl.BlockSpec((None, tf, d), lambda i, e, f: (e, f, 0)),
                 full(g_final)]
    args += [w_gu, w_gu, w_d, g_final]
    scratch = [pltpu.VMEM((tm, d), bf16), pltpu.VMEM((tm, d), f32)]
    if moe:
        scratch += [pltpu.VMEM((tm, d), f32), pltpu.VMEM((tm, LANES), f32)]
    return pl.pallas_call(
        functools.partial(_ffn_kernel, moe=moe, final_norm=final_norm),
        grid=(n // tm, ne, nf),
        in_specs=in_specs,
        out_specs=pl.BlockSpec((tm, d), lambda i, e, f: (i, 0)),
        out_shape=jax.ShapeDtypeStruct((n, d), f32),
        scratch_shapes=scratch,
        compiler_params=_cparams("parallel", "arbitrary", "arbitrary"),
        name="ffn_moe" if moe else "ffn_dense",
    )(*args)


def _lambda_init(layer):
    return 0.8 - 0.6 * math.exp(-0.3 * layer)


def _row(v, width=LANES, rows=8):
    out = jnp.zeros((rows, width), f32)
    for n, x in enumerate(v):
        x = jnp.asarray(x, f32).reshape(-1)
        out = out.at[n, :x.shape[0]].set(x)
    return out


def kernel(x_prompt, x_sample, cache_k, cache_v, cache_logf, cache_mem_k, cache_mem_v, state_wkv, state_shift,
           page_table, mem_prompt, norm_mix, w_in, b_f, diff_lam, diff_g, rw_mu, rw_w0, rw_w2, rw_a0, rw_a2, rw_g2,
           rw_kk, rw_ka, rw_rk, rw_ln_g, rw_ln_b, w_out, norm_mem, mem_norm, w_mem_q, w_mem_kv, w_mem_o, norm_ffn,
           ffn_gu, ffn_down, moe_router, moe_router_b, moe_gu, moe_down, norm_final):
    n_bp, t_p, d = x_prompt.shape
    n_bs, t_s, _ = x_sample.shape
    depth = w_in.shape[0]
    n_pages = page_table.shape[1]
    assert t_s == 4 and (n_pages * PAGE) % MOBA_BLOCK == 0 and n_pages % PAGES_PER_STEP == 0
    assert t_p % MOBA_BLOCK == 0 and d == D_MODEL

    n_pool = cache_k.shape[1]
    k_t = jnp.transpose(cache_k, (0, 1, 3, 4, 2)).reshape(depth, n_pool, ATT_W, PAGE)
    v_t = jnp.transpose(cache_v, (0, 1, 3, 4, 2)).reshape(depth, n_pool, ATT_W, PAGE)
    lf_t = jnp.transpose(cache_logf, (0, 1, 3, 2))
    memk_t = jnp.transpose(cache_mem_k, (0, 1, 3, 4, 2)).reshape(depth, n_bs, GROUP_W, N_MEM)
    memv_t = jnp.transpose(cache_mem_v, (0, 1, 3, 4, 2)).reshape(depth, n_bs, GROUP_W, N_MEM)

    xp = x_prompt.reshape(n_bp * t_p, d)
    xs = x_sample.reshape(n_bs * t_s, d)
    zero_bias = jnp.zeros((1, LANES), f32)
    g_final = norm_final.reshape(1, d)
    outs = {k: [] for k in ("kp", "vp", "lfp", "mkp", "mvp", "wkvp", "shp", "ks", "vs", "lfs", "wkvs", "shs")}

    for l in range(depth):
        lam0 = _lambda_init(l)
        lp = diff_lam[l]
        lam = jnp.exp(jnp.sum(lp[0] * lp[1])) - jnp.exp(jnp.sum(lp[2] * lp[3])) + lam0
        g_pair = jnp.concatenate([diff_g[l], diff_g[l]])
        diff_par = _row([jnp.full((LANES,), lam), g_pair])
        w_l = w_in[l]
        w_cat = jnp.concatenate(
            [w_l[:, 3 * ATT_W + G_HEADS:], w_l[:, :3 * ATT_W],
             jnp.pad(w_l[:, 3 * ATT_W:3 * ATT_W + G_HEADS], ((0, 0), (0, LANES - G_HEADS)))], axis=1).astype(bf16)
        bias_pad = jnp.pad(b_f[l], (0, LANES - G_HEADS)).reshape(1, LANES)
        g_mix = norm_mix[l].reshape(1, d)
        w_out_b = w_out[l].astype(bf16)
        wq_b = w_mem_q[l].astype(bf16)
        wo_b = w_mem_o[l].astype(bf16)
        g_mem = norm_mem[l].reshape(1, d)
        g_ffn = norm_ffn[l].reshape(1, d)
        mu = rw_mu[l].reshape(1, RWKV_W)
        vecs = _row([rw_w0[l], rw_a0[l], rw_kk[l], rw_ka[l], rw_rk[l].reshape(-1)], width=GROUP_W)
        zpad = jnp.zeros((LANES - 64, GROUP_W), f32)
        w2p = jnp.concatenate([rw_w2[l], zpad], axis=0).astype(bf16)
        a2p = jnp.concatenate([zpad, rw_a2[l]], axis=0).astype(bf16)
        g2b = rw_g2[l].astype(bf16)
        ln = _row([rw_ln_g[l], rw_ln_b[l]], width=GROUP_W)
        last = l == depth - 1

        def mixer_common(x2, n_b, t, shift0, m0, tm):
            proj = _norm_matmul(x2, g_mix, w_cat, bias_pad, tm=tm, chunk=1152, logsig_tail=True)
            pr = proj[:, PR_COL:PR_COL + RWKV_W].reshape(n_b, t, RWKV_W)
            prev = jnp.concatenate([shift0[:, None, :], pr[:, :-1]], axis=1).reshape(n_b * t, RWKV_W)
            rw = _rw_prep(proj, prev, mu, vecs, w2p, a2p, g2b, tm)
            return proj, pr, rw

        def ffn(x2, tm):
            if l % 2 == 0:
                i = l // 2
                return _ffn(x2, g_ffn, ffn_gu[i:i + 1].astype(bf16), ffn_down[i:i + 1].astype(bf16), g_final,
                            tm=tm, tf=1408, final_norm=last)
            i = l // 2
            rt = jnp.pad(moe_router[i], ((0, 0), (0, LANES - N_EXPERTS)))
            rb = jnp.pad(moe_router_b[i], (0, LANES - N_EXPERTS)).reshape(1, LANES)
            return _ffn(x2, g_ffn, moe_gu[i].astype(bf16), moe_down[i].astype(bf16), g_final,
                        tm=tm, tf=1408, router=(rt, rb), final_norm=last)

        def wkv_to_bd(s):
            n = s.shape[0]
            m = jnp.zeros((n, GROUP_W, GROUP_W), f32)
            for h in range(G_HEADS):
                m = m.at[:, h * HEAD_DIM:(h + 1) * HEAD_DIM, h * HEAD_DIM:(h + 1) * HEAD_DIM].set(
                    jnp.swapaxes(s[:, h], 1, 2))
            return m

        def bd_to_wkv(m):
            return jnp.stack([jnp.swapaxes(m[:, h * HEAD_DIM:(h + 1) * HEAD_DIM, h * HEAD_DIM:(h + 1) * HEAD_DIM], 1, 2)
                              for h in range(G_HEADS)], axis=1)

        proj, pr, rw = mixer_common(xp, n_bp, t_p, jnp.zeros((n_bp, RWKV_W), f32), None, 512)
        cum = _cumsum_logf(proj, n_bp, t_p)
        ck = jnp.transpose(cum[:, :G_HEADS].reshape(n_bp, t_p, 2, 2), (0, 2, 3, 1)).reshape(n_bp * 2, 2, t_p)
        o_a = _attn_prompt(proj, n_bp, t_p, "diff", 0, diff_par, lam_scale=1.0 - lam0)
        o_b = _attn_prompt(proj, n_bp, t_p, "moba", 1)
        o_c = _attn_prompt(proj, n_bp, t_p, "fox", 2, ck)
        o_d, m_fin = _rw_chunk(rw, jnp.zeros((n_bp, GROUP_W, GROUP_W), f32), ln, n_bp, t_p)
        kv_mem = _norm_matmul(mem_prompt.reshape(n_bp * N_MEM, d), mem_norm[l].reshape(1, d),
                              w_mem_kv[l].astype(bf16), zero_bias, tm=N_MEM, chunk=2 * GROUP_W, logsig_tail=False)
        mk_p = kv_mem[:, :GROUP_W].reshape(n_bp, N_MEM, GROUP_W)
        mv_p = kv_mem[:, GROUP_W:].reshape(n_bp, N_MEM, GROUP_W)
        to3 = lambda a: a.reshape(n_bp, t_p, a.shape[-1])
        x3 = _post_mixer(to3(xp), [to3(o_a), to3(o_b), to3(o_c), to3(o_d)], w_out_b, g_mem, wq_b,
                         jnp.swapaxes(mk_p, 1, 2), jnp.swapaxes(mv_p, 1, 2), wo_b, 512)
        xp = ffn(x3.reshape(n_bp * t_p, d), 512)
        outs["kp"].append(proj[:, K_COL:K_COL + ATT_W].reshape(n_bp, t_p, 12, HEAD_DIM))
        outs["vp"].append(proj[:, V_COL:V_COL + ATT_W].reshape(n_bp, t_p, 12, HEAD_DIM))
        outs["lfp"].append(proj[:, FL_COL:FL_COL + G_HEADS].reshape(n_bp, t_p, G_HEADS))
        outs["mkp"].append(mk_p.reshape(n_bp, N_MEM, G_HEADS, HEAD_DIM))
        outs["mvp"].append(mv_p.reshape(n_bp, N_MEM, G_HEADS, HEAD_DIM))
        outs["wkvp"].append(bd_to_wkv(m_fin))
        outs["shp"].append(pr[:, -1])

        proj, pr, rw = mixer_common(xs, n_bs, t_s, state_shift[l], None, n_bs * t_s)
        k_new = proj[:, K_COL:K_COL + ATT_W].reshape(n_bs, t_s, ATT_W)
        v_new = proj[:, V_COL:V_COL + ATT_W].reshape(n_bs, t_s, ATT_W)
        lf_new = proj[:, FL_COL:FL_COL + G_HEADS].reshape(n_bs, t_s, G_HEADS)
        pad_t = lambda a: jnp.pad(jnp.swapaxes(a, 1, 2), ((0, 0), (0, 0), (0, PAGE - t_s)))
        o_abc = _decode_attn(page_table, proj[:, Q_COL:Q_COL + ATT_W].reshape(n_bs, t_s, ATT_W), pad_t(k_new), pad_t(v_new),
                             pad_t(lf_new), diff_par, k_t, v_t, lf_t, l, 1.0 - lam0)
        rw_pad = jnp.pad(rw.reshape(n_bs, t_s, 8 * GROUP_W), ((0, 0), (0, RW_CHUNK - t_s), (0, 0)))
        o_d, m_fin = _rw_chunk(rw_pad.reshape(n_bs * RW_CHUNK, 8 * GROUP_W), wkv_to_bd(state_wkv[l]), ln, n_bs, RW_CHUNK)
        o_d = o_d.reshape(n_bs, RW_CHUNK, GROUP_W)[:, :t_s]
        x3 = _post_mixer(xs.reshape(n_bs, t_s, d), [o_abc, o_d], w_out_b, g_mem, wq_b, memk_t[l], memv_t[l], wo_b, t_s)
        xs = ffn(x3.reshape(n_bs * t_s, d), n_bs * t_s)
        outs["ks"].append(k_new.reshape(n_bs, t_s, 12, HEAD_DIM))
        outs["vs"].append(v_new.reshape(n_bs, t_s, 12, HEAD_DIM))
        outs["lfs"].append(lf_new)
        outs["wkvs"].append(bd_to_wkv(m_fin))
        outs["shs"].append(pr[:, -1])

    st = lambda k: jnp.stack(outs[k])
    return (xp.reshape(n_bp, t_p, d), xs.reshape(n_bs, t_s, d), st("kp"), st("vp"), st("lfp"), st("mkp"), st("mvp"),
            st("wkvp"), st("shp"), st("ks"), st("vs"), st("lfs"), st("wkvs"), st("shs"))
```

```python
import functools
import math

import jax
import jax.numpy as jnp
from jax import lax
from jax.experimental import pallas as pl
from jax.experimental.pallas import tpu as pltpu

f32 = jnp.float32
bf16 = jnp.bfloat16

D_MODEL = 1024
HEAD_DIM = 64
G_HEADS = 4
GROUP_W = G_HEADS * HEAD_DIM
ATT_W = 3 * GROUP_W
RWKV_W = 1024
IN_W = 3 * ATT_W + G_HEADS + RWKV_W
PROJ_W = RWKV_W + 3 * ATT_W + 128
PR_COL = 0
Q_COL = RWKV_W
K_COL = Q_COL + ATT_W
V_COL = K_COL + ATT_W
FL_COL = V_COL + ATT_W
MOBA_BLOCK = 256
MOBA_TOPK = 3
PAGE = 128
N_MEM = 256
D_FF = 2816
N_EXPERTS = 8
NORM_EPS = 1e-6
GN_EPS = 64e-5
NEG = -1e30
LANES = 128
RW_CHUNK = 64
VMEM_LIMIT = 56 * 1024 * 1024

HI = lax.Precision.HIGHEST
LOG2E = 1.4426950408889634


def _cparams(*sem):
    return pltpu.CompilerParams(dimension_semantics=sem, vmem_limit_bytes=VMEM_LIMIT)


def _dot(a, b, prec=None):
    return jnp.dot(a, b, preferred_element_type=f32, precision=prec)


def _dot_nt(a, b, prec=None):
    return lax.dot_general(a, b, (((1,), (1,)), ((), ())), preferred_element_type=f32, precision=prec)


def _dot_tn(a, b, prec=None):
    return lax.dot_general(a, b, (((0,), (0,)), ((), ())), preferred_element_type=f32, precision=prec)


def _rms(x, g):
    return x * lax.rsqrt(jnp.mean(x * x, axis=-1, keepdims=True) + NORM_EPS) * g


def _pair_rmsnorm(o, g_pair):
    lane = lax.broadcasted_iota(jnp.int32, o.shape, 1)
    lo = lane < HEAD_DIM
    sq = o * o
    s_lo = jnp.sum(jnp.where(lo, sq, 0.0), axis=-1, keepdims=True)
    s_hi = jnp.sum(jnp.where(lo, 0.0, sq), axis=-1, keepdims=True)
    ms = jnp.where(lo, s_lo, s_hi) * (1.0 / HEAD_DIM)
    return o * lax.rsqrt(ms + NORM_EPS) * g_pair


def _norm_matmul_kernel(*refs, chunk, logsig_tail, n_t):
    x_ref, g_ref, w_ref, b_ref = refs[:4]
    wt_ref = refs[4] if n_t else None
    o_ref = refs[4 + bool(n_t)]
    t_refs = refs[5 + bool(n_t):]
    h = _rms(x_ref[...], g_ref[...]).astype(bf16)
    n_out = o_ref.shape[1]
    for c0 in range(0, n_out, chunk):
        c1 = min(c0 + chunk, n_out)
        y = _dot(h, w_ref[:, c0:c1])
        if logsig_tail and c1 == n_out:
            t0 = c1 - c0 - LANES
            tail = jax.nn.log_sigmoid(y[:, t0:] + b_ref[...])
            o_ref[:, c0:c0 + t0] = y[:, :t0]
            o_ref[:, c0 + t0:c1] = tail
        else:
            o_ref[:, c0:c1] = y
    if n_t:
        qt_ref, kt_ref, vt_ref, kb_ref, vtb_ref = t_refs
        qt_ref[...] = _dot_nt(wt_ref[0:ATT_W, :], h)
        kt_ref[...] = _dot_nt(wt_ref[ATT_W:2 * ATT_W, :], h)
        vt = _dot_nt(wt_ref[2 * ATT_W:, :], h)
        vt_ref[...] = vt
        vtb_ref[...] = vt.astype(bf16)
        kb_ref[...] = o_ref[:, K_COL:K_COL + ATT_W].astype(bf16)


def _norm_matmul(x, g, w, bias_pad, *, tm, chunk, logsig_tail, w_t=None, n_b=1):
    n, d = x.shape
    n_out = w.shape[1]
    n_t = 0 if w_t is None else w_t.shape[0] // ATT_W
    in_specs = [pl.BlockSpec((tm, d), lambda i: (i, 0)),
                pl.BlockSpec((1, d), lambda i: (0, 0)),
                pl.BlockSpec((d, n_out), lambda i: (0, 0)),
                pl.BlockSpec((1, LANES), lambda i: (0, 0))]
    args = [x, g, w, bias_pad]
    out_specs = [pl.BlockSpec((tm, n_out), lambda i: (i, 0))]
    out_shape = [jax.ShapeDtypeStruct((n, n_out), f32)]
    if n_t:
        t = n // n_b
        nt = t // tm
        in_specs.append(pl.BlockSpec(w_t.shape, lambda i: (0, 0)))
        args.append(w_t)
        t_spec = pl.BlockSpec((None, ATT_W, tm), lambda i: (i // nt, 0, i % nt))
        out_specs += [t_spec] * 3 + [pl.BlockSpec((tm, ATT_W), lambda i: (i, 0)), t_spec]
        out_shape += ([jax.ShapeDtypeStruct((n_b, ATT_W, t), f32)] * 3
                      + [jax.ShapeDtypeStruct((n, ATT_W), bf16), jax.ShapeDtypeStruct((n_b, ATT_W, t), bf16)])
    res = pl.pallas_call(
        functools.partial(_norm_matmul_kernel, chunk=chunk, logsig_tail=logsig_tail, n_t=n_t),
        grid=(n // tm,),
        in_specs=in_specs,
        out_specs=out_specs,
        out_shape=out_shape,
        compiler_params=_cparams("parallel"),
        name="norm_matmul",
    )(*args)
    return res if n_t else res[0]


def _cumsum_kernel(lf_ref, o_ref, carry_ref, *, tc):
    @pl.when(pl.program_id(1) == 0)
    def _():
        carry_ref[...] = jnp.zeros_like(carry_ref)
    r = lax.broadcasted_iota(jnp.int32, (tc, tc), 0)
    c = lax.broadcasted_iota(jnp.int32, (tc, tc), 1)
    tri = jnp.where(c <= r, 1.0, 0.0).astype(f32)
    cum = _dot(tri, lf_ref[...], HI) + carry_ref[...]
    carry_ref[...] = cum[tc - 1:tc, :]
    neg = cum * (-LOG2E)
    hi = neg.astype(bf16).astype(f32)
    mid = (neg - hi).astype(bf16).astype(f32)
    lo = (neg - hi - mid).astype(bf16).astype(f32)
    src = lax.broadcasted_iota(jnp.int32, (LANES, 2 * LANES), 0)
    dst = lax.broadcasted_iota(jnp.int32, (LANES, 2 * LANES), 1)
    out = jnp.zeros((tc, 2 * LANES), f32)
    for k, piece in enumerate((hi, mid, lo)):
        place = jnp.where((src < G_HEADS) & (dst == LANES * (src // 2) + 3 * (src % 2) + k), 1.0, 0.0)
        out = out + _dot(piece, place, HI)
    o_ref[...] = out.astype(bf16)


def _cumsum_logf(proj, n_b, t):
    tc = 256
    nt = t // tc
    return pl.pallas_call(
        functools.partial(_cumsum_kernel, tc=tc),
        grid=(n_b, nt),
        in_specs=[pl.BlockSpec((tc, LANES), lambda b, i: (b * nt + i, FL_COL // LANES))],
        out_specs=pl.BlockSpec((tc, 2 * LANES), lambda b, i: (b * nt + i, 0)),
        out_shape=jax.ShapeDtypeStruct((n_b * t, 2 * LANES), bf16),
        scratch_shapes=[pltpu.VMEM((1, LANES), f32)],
        compiler_params=_cparams("parallel", "arbitrary"),
        name="cumsum_logf",
    )(proj)


def _attn_kernel(*refs, mode, tq, tk, lam_scale):
    if mode == "diff":
        qt_ref, k_ref, vt_ref, par_ref, g_ref, o_ref = refs[:6]
    elif mode == "fox":
        qt_ref, k_ref, vt_ref, kaug_ref, o_ref = refs[:5]
    else:
        qt_ref, k_ref, vt_ref, o_ref = refs[:4]
    n_in = {"diff": 6, "fox": 5, "moba": 4}[mode]
    m_s, l_s, acc_s = refs[n_in:n_in + 2], refs[n_in + 2:n_in + 4], refs[n_in + 4:n_in + 6]
    if mode == "moba":
        kmean_s, sel_s = refs[n_in + 6], refs[n_in + 7:n_in + 9]
    i = pl.program_id(1)
    nvar = 4 if mode == "diff" else 2
    cols = nvar * tq
    drow = lax.broadcasted_iota(jnp.int32, (LANES, tq), 0)
    qpos = i * tq + (lax.broadcasted_iota(jnp.int32, (1, cols), 1) & (tq - 1))
    qs = []
    qs32 = []
    for hp in range(2):
        qt = qt_ref[hp * LANES:(hp + 1) * LANES, :]
        if mode == "diff":
            half = HEAD_DIM // 2
            scale = half ** -0.5 * LOG2E
            parts = [jnp.where((drow >= v * half) & (drow < (v + 1) * half), qt, 0.0) for v in range(4)]
        else:
            scale = HEAD_DIM ** -0.5 * LOG2E
            parts = [jnp.where(drow < HEAD_DIM, qt, 0.0), jnp.where(drow >= HEAD_DIM, qt, 0.0)]
        q32 = jnp.concatenate(parts, axis=1) * scale
        qs32.append(q32)
        if mode == "fox":
            r = lax.broadcasted_iota(jnp.int32, (LANES, cols), 0)
            c = lax.broadcasted_iota(jnp.int32, (LANES, cols), 1)
            ind = jnp.where(((r < 3) & (c < tq)) | ((r >= 3) & (r < 6) & (c >= tq)), 1.0, 0.0)
            qs.append(jnp.concatenate([q32, ind], axis=0).astype(bf16))
        else:
            qs.append(q32.astype(bf16))
        m_s[hp][...] = jnp.full(m_s[hp].shape, NEG, f32)
        l_s[hp][...] = jnp.zeros(l_s[hp].shape, f32)
        acc_s[hp][...] = jnp.zeros(acc_s[hp].shape, f32)

    if mode == "moba":
        nb = k_ref.shape[0] // MOBA_BLOCK

        @pl.when(i == 0)
        def _():
            kmean_s[...] = jnp.zeros(kmean_s.shape, f32)

            def blk(n, _):
                kb = k_ref[pl.ds(pl.multiple_of(n * MOBA_BLOCK, MOBA_BLOCK), MOBA_BLOCK), :]
                kmean_s[pl.ds(n, 1), :] = jnp.mean(kb.astype(f32), axis=0, keepdims=True)
                return 0
            lax.fori_loop(0, nb, blk, 0)

        nidx = lax.broadcasted_iota(jnp.int32, (LANES, cols), 0)
        nidx_f = nidx.astype(f32)
        own = qpos // MOBA_BLOCK
        own_f = own.astype(f32)
        for hp in range(2):
            gate = _dot(kmean_s[:, hp * LANES:(hp + 1) * LANES], qs32[hp], HI)
            g = jnp.where(nidx < own, gate, NEG)
            sel = jnp.zeros((LANES, cols), f32)
            for _ in range(MOBA_TOPK):
                mx = jnp.max(g, axis=0, keepdims=True)
                first = jnp.min(jnp.where(g == mx, nidx_f, 1e9), axis=0, keepdims=True)
                pick = nidx_f == first
                sel = jnp.where(pick & (first < own_f), 1.0, sel)
                g = jnp.where(pick, -3e38, g)
            sel_s[hp][...] = sel

    def step(j, diagonal):
        start = pl.multiple_of(j * tk, tk)
        for hp in range(2):
            lanes = slice(hp * LANES, (hp + 1) * LANES)
            kb = k_ref[pl.ds(start, tk), lanes]
            if mode == "fox":
                kb = jnp.concatenate([kb, kaug_ref[pl.ds(start, tk), lanes]], axis=1)
            s = _dot(kb, qs[hp])
            valid = None
            if mode == "moba" and not diagonal:
                valid = sel_s[hp][pl.ds(j, 1), :] > 0.5
            if diagonal:
                kpos = start + lax.broadcasted_iota(jnp.int32, (tk, 1), 0)
                valid = kpos <= qpos
            if valid is not None:
                s = jnp.where(valid, s, NEG)
            m_old = m_s[hp][...]
            m_new = jnp.maximum(m_old, jnp.max(s, axis=0, keepdims=True))
            alpha = jnp.exp2(m_old - m_new)
            p = jnp.exp2(s - m_new)
            if valid is not None:
                p = jnp.where(valid, p, 0.0)
            l_s[hp][...] = alpha * l_s[hp][...] + jnp.sum(p, axis=0, keepdims=True)
            vt = vt_ref[lanes, pl.ds(start, tk)]
            acc_s[hp][...] = alpha * acc_s[hp][...] + _dot(vt, p.astype(bf16))
            m_s[hp][...] = m_new

    n_full = (i * tq) // tk

    def body(j, _):
        step(j, False)
        return 0
    lax.fori_loop(0, n_full, body, 0)
    step(n_full, True)

    lo = drow < HEAD_DIM
    for hp in range(2):
        a = acc_s[hp][...] * (1.0 / l_s[hp][...])
        if mode == "diff":
            lam = par_ref[0:1, 0:1]
            o = jnp.where(lo, a[:, 0:tq] - lam * a[:, tq:2 * tq], a[:, 2 * tq:3 * tq] - lam * a[:, 3 * tq:4 * tq])
            sq = o * o
            s_lo = jnp.sum(jnp.where(lo, sq, 0.0), axis=0, keepdims=True)
            s_hi = jnp.sum(jnp.where(lo, 0.0, sq), axis=0, keepdims=True)
            ms = jnp.where(lo, s_lo, s_hi) * (1.0 / HEAD_DIM)
            o = o * lax.rsqrt(ms + NORM_EPS) * g_ref[...] * lam_scale
        else:
            o = jnp.where(lo, a[:, 0:tq], a[:, tq:2 * tq])
        o_ref[:, hp * LANES:(hp + 1) * LANES] = o.T


def _attn_prompt(k_b, q_t, v_t, n_b, t, mode, group, extra=None, lam_scale=1.0):
    tk = MOBA_BLOCK if mode == "moba" else 512
    tq = 128 if mode == "diff" else 256
    nq = t // tq
    nvar = 4 if mode == "diff" else 2
    cols = nvar * tq
    in_specs = [pl.BlockSpec((None, GROUP_W, tq), lambda b, i: (b, group, i)),
                pl.BlockSpec((t, GROUP_W), lambda b, i: (b, group)),
                pl.BlockSpec((None, GROUP_W, t), lambda b, i: (b, group, 0))]
    args = [q_t, k_b, v_t]
    scratch = ([pltpu.VMEM((1, cols), f32)] * 4 + [pltpu.VMEM((LANES, cols), f32)] * 2)
    if mode == "diff":
        in_specs += [pl.BlockSpec((8, LANES), lambda b, i: (0, 0)),
                     pl.BlockSpec((LANES, tq), lambda b, i: (0, 0))]
        args += list(extra)
    elif mode == "fox":
        in_specs.append(pl.BlockSpec((t, GROUP_W), lambda b, i: (b, 0)))
        args.append(extra)
    else:
        scratch += [pltpu.VMEM((LANES, GROUP_W), f32)] + [pltpu.VMEM((LANES, cols), f32)] * 2
    return pl.pallas_call(
        functools.partial(_attn_kernel, mode=mode, tq=tq, tk=tk, lam_scale=lam_scale),
        grid=(n_b, nq),
        in_specs=in_specs,
        out_specs=pl.BlockSpec((tq, GROUP_W), lambda b, i: (b * nq + i, 0)),
        out_shape=jax.ShapeDtypeStruct((n_b * t, GROUP_W), f32),
        scratch_shapes=scratch,
        compiler_params=_cparams("parallel", "arbitrary"),
        name="attn_" + mode,
    )(*args)


def _head_ones(n):
    r = lax.broadcasted_iota(jnp.int32, (n, n), 0) // HEAD_DIM
    c = lax.broadcasted_iota(jnp.int32, (n, n), 1) // HEAD_DIM
    return jnp.where(r == c, 1.0, 0.0).astype(f32)


def _rw_prep_kernel(pr_ref, prev_ref, mu_ref, vec_ref, w2_ref, a2_ref, g2_ref, o_ref):
    pr = pr_ref[...]
    xm = pr + (prev_ref[...] - pr) * mu_ref[...]
    r = xm[:, 0:GROUP_W]
    k = xm[:, GROUP_W:2 * GROUP_W]
    v = xm[:, 2 * GROUP_W:3 * GROUP_W]
    x_lora = xm[:, 3 * GROUP_W:3 * GROUP_W + LANES]
    x_gate = xm[:, 3 * GROUP_W + LANES:]
    w0, a0, kk_p, ka, rk = (vec_ref[n:n + 1, :] for n in range(5))
    w_lin = w0 + _dot(jnp.tanh(x_lora).astype(bf16), w2_ref[...])
    w_log = -jax.nn.softplus(-w_lin) - 0.5
    lw = -jnp.exp(w_log)
    a = jax.nn.sigmoid(a0 + _dot(x_lora.astype(bf16), a2_ref[...]))
    g = _dot(jax.nn.sigmoid(x_gate).astype(bf16), g2_ref[...])
    ones = _head_ones(GROUP_W)
    kk = k * kk_p
    kk = kk * lax.rsqrt(_dot(kk * kk, ones, HI) + 1e-12)
    kh = k * (1.0 + (a - 1.0) * ka)
    bonus = _dot(r * kh * rk, ones, HI) * v
    for n, val in enumerate((r, lw, kh, v, kk, kk * a, g, bonus)):
        o_ref[:, n * GROUP_W:(n + 1) * GROUP_W] = val


def _rw_prep(proj, prev, mu, vecs, w2p, a2p, g2, tm):
    n = proj.shape[0]
    full = lambda shape: pl.BlockSpec(shape, lambda i: (0,) * len(shape))
    return pl.pallas_call(
        _rw_prep_kernel,
        grid=(n // tm,),
        in_specs=[pl.BlockSpec((tm, RWKV_W), lambda i: (i, PR_COL // RWKV_W)),
                  pl.BlockSpec((tm, RWKV_W), lambda i: (i, 0)),
                  full((1, RWKV_W)), full((8, GROUP_W)), full((LANES, GROUP_W)), full((LANES, GROUP_W)),
                  full((LANES, GROUP_W))],
        out_specs=pl.BlockSpec((tm, 8 * GROUP_W), lambda i: (i, 0)),
        out_shape=jax.ShapeDtypeStruct((n, 8 * GROUP_W), f32),
        compiler_params=_cparams("parallel"),
        name="rw_prep",
    )(proj, prev, mu, vecs, w2p, a2p, g2)


def _rw_chunk_kernel(x_ref, m0_ref, ln_ref, o_ref, mfin_ref, m_s):
    c = pl.program_id(1)
    nc = pl.num_programs(1)
    C = RW_CHUNK
    W = GROUP_W

    @pl.when(c == 0)
    def _():
        m_s[...] = m0_ref[...]

    r, lw, kh, v, kk, b, g, bonus = (x_ref[:, n * W:(n + 1) * W] for n in range(8))

    tr = lax.broadcasted_iota(jnp.int32, (C, C), 0)
    tc = lax.broadcasted_iota(jnp.int32, (C, C), 1)
    cum = _dot(jnp.where(tc <= tr, 1.0, 0.0).astype(f32), lw, HI)
    g_inc = jnp.exp(cum)
    inv = jnp.exp(-cum)
    g_end = jnp.exp(cum[C - 1:C, :])

    ri = lax.broadcasted_iota(jnp.int32, (W, W), 0)
    ci = lax.broadcasted_iota(jnp.int32, (W, W), 1)
    same = (ri // C) == (ci // HEAD_DIM)
    same_t = (ri // C) == (ci // C)
    strict = same_t & ((ci % C) < (ri % C))
    incl = same_t & ((ci % C) <= (ri % C))

    def sm(x):
        return jnp.where(same, jnp.concatenate([x] * G_HEADS, axis=0), 0.0)

    a_sm = sm(-kk * jnp.exp(cum - lw))
    r_sm = sm(r * g_inc)
    b_sm = sm(b * inv)
    k_sm = sm(kh * inv)
    v_sm = sm(v)

    def mm(x, y):
        return _dot(x.astype(bf16), y.astype(bf16))

    gram = _dot_nt(jnp.concatenate([a_sm, r_sm], axis=0).astype(bf16),
                   jnp.concatenate([b_sm, k_sm], axis=0).astype(bf16))
    n_ab = jnp.where(strict, gram[:W, :W], 0.0)
    a_ak = jnp.where(strict, gram[:W, W:], 0.0)
    a_rb = jnp.where(incl, gram[W:, :W], 0.0)
    a_rk = jnp.where(incl, gram[W:, W:], 0.0)

    eye = jnp.where(ri == ci, 1.0, 0.0).astype(f32)
    t_inv = eye + n_ab
    n_pow = n_ab
    for _ in range(int(math.log2(C)) - 1):
        n_pow = mm(n_pow, n_pow)
        t_inv = t_inv + mm(n_pow, t_inv)

    akv = mm(a_ak, v_sm)
    tx = mm(t_inv, jnp.concatenate([a_sm, akv], axis=1))
    y = mm(a_rb, tx)
    w_o = r_sm + y[:, :W]
    o_0 = y[:, W:] + mm(a_rk, v_sm)
    bt = (b_sm * g_end).astype(bf16)
    kt = (k_sm * g_end).astype(bf16)
    pq = _dot_tn(bt, tx.astype(bf16))
    p_mat = jnp.where(ri == ci, g_end, 0.0) + pq[:, :W]
    q_mat = pq[:, W:] + _dot_tn(kt, v_sm.astype(bf16))

    m_prev = m_s[...]
    o_sm = _dot(w_o, m_prev, HI) + o_0
    m_new = _dot(p_mat, m_prev, HI) + q_mat
    m_s[...] = m_new
    o = o_sm[0:C] + o_sm[C:2 * C] + o_sm[2 * C:3 * C] + o_sm[3 * C:4 * C]

    mean_mat = _head_ones(W) * (1.0 / HEAD_DIM)
    mu = _dot(o, mean_mat, HI)
    d = o - mu
    var = _dot(d * d, mean_mat, HI)
    o_n = d * lax.rsqrt(var + GN_EPS) * ln_ref[0:1, :] + ln_ref[1:2, :]
    o_ref[...] = (o_n + bonus) * g

    @pl.when(c == nc - 1)
    def _():
        mfin_ref[...] = m_new


def _rw_chunk(rw, m0, ln, n_b, t):
    nc = t // RW_CHUNK
    return pl.pallas_call(
        _rw_chunk_kernel,
        grid=(n_b, nc),
        in_specs=[pl.BlockSpec((RW_CHUNK, 8 * GROUP_W), lambda b, c: (b * nc + c, 0)),
                  pl.BlockSpec((None, GROUP_W, GROUP_W), lambda b, c: (b, 0, 0)),
                  pl.BlockSpec((8, GROUP_W), lambda b, c: (0, 0))],
        out_specs=[pl.BlockSpec((RW_CHUNK, GROUP_W), lambda b, c: (b * nc + c, 0)),
                   pl.BlockSpec((None, GROUP_W, GROUP_W), lambda b, c: (b, 0, 0))],
        out_shape=[jax.ShapeDtypeStruct((n_b * t, GROUP_W), f32),
                   jax.ShapeDtypeStruct((n_b, GROUP_W, GROUP_W), f32)],
        scratch_shapes=[pltpu.VMEM((GROUP_W, GROUP_W), f32)],
        compiler_params=_cparams("parallel", "arbitrary"),
        name="rw_chunk",
    )(rw, m0, ln)


N_COLS = 64
COL_B, COL_C = 32, 48
PAGES_PER_STEP = 8


def _col_maps(shape, axis):
    c = lax.broadcasted_iota(jnp.int32, shape, axis)
    tok = c % 4
    head = jnp.where(c < COL_B, c // 8, jnp.where(c < COL_C, 4 + (c - COL_B) // 4, 8 + (c - COL_C) // 4))
    half = (c // 4) % 2
    lo = head * HEAD_DIM + jnp.where(c < COL_B, half * (HEAD_DIM // 2), 0)
    hi = jnp.where(c < COL_B, lo + HEAD_DIM // 2, lo + HEAD_DIM)
    return c, tok, head, lo, hi


def _decode_kernel(*refs, lam_scale, nj):
    npg = PAGES_PER_STEP
    pt_ref, q_ref, kn_ref, vn_ref, ln_ref, par_ref = refs[:6]
    k_refs, v_refs, l_refs = refs[6:6 + npg], refs[6 + npg:6 + 2 * npg], refs[6 + 2 * npg:6 + 3 * npg]
    o_ref, qbd_s, m_s, l_s, acc_s, carry_s, bm_s, bl_s, bg_s, bacc_s = refs[6 + 3 * npg:]
    j = pl.program_id(1)
    nblk = npg // 2
    c1, tok1, head1, lo1, hi1 = _col_maps((N_COLS, 1), 0)

    @pl.when(j == 0)
    def _():
        q4 = q_ref[...]
        lanes = lax.broadcasted_iota(jnp.int32, (N_COLS, ATT_W), 1)
        qb = jnp.zeros((N_COLS, ATT_W), f32)
        for t in range(4):
            qb = jnp.where(tok1 == t, q4[t:t + 1, :], qb)
        scale = jnp.where(c1 < COL_B, (HEAD_DIM // 2) ** -0.5, HEAD_DIM ** -0.5)
        qbd_s[...] = (jnp.where((lanes >= lo1) & (lanes < hi1), qb, 0.0) * scale).astype(bf16)
        m_s[...] = jnp.full(m_s.shape, NEG, f32)
        l_s[...] = jnp.zeros(l_s.shape, f32)
        acc_s[...] = jnp.zeros(acc_s.shape, f32)
        carry_s[...] = jnp.zeros(carry_s.shape, f32)
        bm_s[...] = jnp.zeros(bm_s.shape, f32)
        bl_s[...] = jnp.zeros(bl_s.shape, f32)
        bg_s[...] = jnp.zeros(bg_s.shape, f32)

    qbd = qbd_s[...]
    is_c = c1 >= COL_C

    def forget_bias(cum):
        bias = jnp.zeros((N_COLS, cum.shape[1]), f32)
        for h in range(G_HEADS):
            bias = jnp.where(is_c & (head1 == 8 + h), -cum[h:h + 1, :], bias)
        return bias

    def upper(n):
        r = lax.broadcasted_iota(jnp.int32, (n, n), 0)
        c = lax.broadcasted_iota(jnp.int32, (n, n), 1)
        return jnp.where(r <= c, 1.0, 0.0).astype(f32)

    def merge(parts):
        m_old = m_s[...]
        m_new = m_old
        for m_k, _, _ in parts:
            m_new = jnp.maximum(m_new, m_k)
        a = jnp.exp(m_old - m_new)
        l_new = a * l_s[...]
        acc_new = a * acc_s[...]
        for m_k, l_k, acc_k in parts:
            e_k = jnp.exp(m_k - m_new)
            l_new = l_new + e_k * l_k
            acc_new = acc_new + e_k * acc_k
        l_s[...] = l_new
        acc_s[...] = acc_new
        m_s[...] = m_new

    kw = 2 * PAGE
    lf = jnp.concatenate([jnp.concatenate([l_refs[2 * k][...], l_refs[2 * k + 1][...]], axis=1) for k in range(nblk)],
                         axis=0)
    cum_loc = _dot(lf, upper(kw), HI)
    carry = carry_s[0:G_HEADS, 0:1]
    blane = lax.broadcasted_iota(jnp.int32, (COL_C - COL_B, LANES), 1)
    parts = []
    for k in range(nblk):
        cum = cum_loc[G_HEADS * k:G_HEADS * (k + 1), :] + carry
        carry = cum[:, kw - 1:kw]
        s = jnp.concatenate([_dot(qbd, k_refs[2 * k][...].astype(bf16)),
                             _dot(qbd, k_refs[2 * k + 1][...].astype(bf16))], axis=1)
        gate = jnp.sum(s[COL_B:COL_C], axis=-1, keepdims=True) * (1.0 / MOBA_BLOCK)
        s = s + forget_bias(cum)
        m_blk = jnp.max(s, axis=-1, keepdims=True)
        p = jnp.exp(s - m_blk)
        l_blk = jnp.sum(p, axis=-1, keepdims=True)
        pb = p.astype(bf16)
        acc_blk = (_dot_nt(pb[:, :PAGE], v_refs[2 * k][...].astype(bf16))
                   + _dot_nt(pb[:, PAGE:], v_refs[2 * k + 1][...].astype(bf16)))
        parts.append((m_blk, l_blk, acc_blk))
        n = j * nblk + k
        bm_s[...] = jnp.where(blane == n, m_blk[COL_B:COL_C], bm_s[...])
        bl_s[...] = jnp.where(blane == n, l_blk[COL_B:COL_C], bl_s[...])
        bg_s[...] = jnp.where(blane == n, gate, bg_s[...])
        bacc_s[n] = acc_blk[COL_B:COL_C]
    carry_s[0:G_HEADS, :] = jnp.broadcast_to(carry, (G_HEADS, LANES))
    merge(parts)

    @pl.when(j == nj - 1)
    def _():
        sn = _dot(qbd, kn_ref[...].astype(bf16))
        cum_n = _dot(ln_ref[...], upper(PAGE), HI) + carry_s[0:G_HEADS, 0:1]
        klane = lax.broadcasted_iota(jnp.int32, (N_COLS, PAGE), 1)
        sn = jnp.where(klane <= tok1, sn + forget_bias(cum_n), NEG)
        m_n = jnp.max(sn, axis=-1, keepdims=True)
        p_n = jnp.exp(sn - m_n)
        l_n = jnp.sum(p_n, axis=-1, keepdims=True)
        acc_n = _dot_nt(p_n.astype(bf16), vn_ref[...].astype(bf16))
        merge([(m_n, l_n, acc_n)])

        nb = nj * nblk
        bgate = jnp.where(blane < nb, bg_s[...], NEG)
        blane_f = blane.astype(f32)
        sel = jnp.zeros(bgate.shape, f32)
        for _ in range(MOBA_TOPK):
            mx = jnp.max(bgate, axis=-1, keepdims=True)
            first = jnp.min(jnp.where(bgate == mx, blane_f, 1e9), axis=-1, keepdims=True)
            pick = blane_f == first
            sel = jnp.where(pick & (first < nb), 1.0, sel)
            bgate = jnp.where(pick, -3e38, bgate)
        on = sel > 0.5
        mb_n, lb_n, accb_n = m_n[COL_B:COL_C], l_n[COL_B:COL_C], acc_n[COL_B:COL_C]
        m_tot = jnp.maximum(jnp.max(jnp.where(on, bm_s[...], NEG), axis=-1, keepdims=True), mb_n)
        wgt = jnp.where(on, jnp.exp(bm_s[...] - m_tot), 0.0)
        e_n = jnp.exp(mb_n - m_tot)
        l_b = jnp.sum(wgt * bl_s[...], axis=-1, keepdims=True) + e_n * lb_n

        def add_block(n, acc):
            w_n = jnp.sum(jnp.where(blane == n, wgt, 0.0), axis=-1, keepdims=True)
            return acc + w_n * bacc_s[n]
        acc_b = lax.fori_loop(0, nb, add_block, e_n * accb_n)
        l_s[COL_B:COL_C, :] = l_b
        acc_s[COL_B:COL_C, :] = acc_b

        lam = par_ref[0:1, 0:1]
        coef = jnp.where((c1 < COL_B) & ((c1 // 4) % 2 == 1), -lam, 1.0) / l_s[...]
        lanes = lax.broadcasted_iota(jnp.int32, (N_COLS, ATT_W), 1)
        keep = (lanes >= head1 * HEAD_DIM) & (lanes < (head1 + 1) * HEAD_DIM)
        contrib = jnp.where(keep, acc_s[...] * coef, 0.0)
        _, tok_r, _, _, _ = _col_maps((8, N_COLS), 1)
        trow = lax.broadcasted_iota(jnp.int32, (8, N_COLS), 0)
        o_tok = _dot(jnp.where(tok_r == trow, 1.0, 0.0).astype(f32), contrib, HI)[0:4]
        o_ref[:, 2 * LANES:] = o_tok[:, 2 * LANES:]
        for hp in range(2):
            o_ref[:, hp * LANES:(hp + 1) * LANES] = (
                _pair_rmsnorm(o_tok[:, hp * LANES:(hp + 1) * LANES], par_ref[1:2, :]) * lam_scale)


def _decode_attn(page_table, q_s, kn_t, vn_t, ln_t, par, k_t, v_t, lf_t, layer, lam_scale):
    n_b, n_pages = page_table.shape
    npg = PAGES_PER_STEP
    nj = n_pages // npg
    nblocks = n_pages // 2

    def page_spec(rows, g):
        return pl.BlockSpec((None, None, rows, PAGE), lambda b, j, pt: (layer, pt[b, npg * j + g], 0, 0))
    new_spec = lambda rows: pl.BlockSpec((None, rows, PAGE), lambda b, j, pt: (b, 0, 0))
    nbc = COL_C - COL_B
    return pl.pallas_call(
        functools.partial(_decode_kernel, lam_scale=lam_scale, nj=nj),
        grid_spec=pltpu.PrefetchScalarGridSpec(
            num_scalar_prefetch=1,
            grid=(n_b, nj),
            in_specs=([pl.BlockSpec((None, 4, ATT_W), lambda b, j, pt: (b, 0, 0)),
                       new_spec(ATT_W), new_spec(ATT_W), new_spec(G_HEADS),
                       pl.BlockSpec((8, LANES), lambda b, j, pt: (0, 0))]
                      + [page_spec(ATT_W, g) for g in range(npg)] * 2
                      + [page_spec(G_HEADS, g) for g in range(npg)]),
            out_specs=pl.BlockSpec((None, 4, ATT_W), lambda b, j, pt: (b, 0, 0)),
            scratch_shapes=[pltpu.VMEM((N_COLS, ATT_W), bf16),
                            pltpu.VMEM((N_COLS, 1), f32), pltpu.VMEM((N_COLS, 1), f32),
                            pltpu.VMEM((N_COLS, ATT_W), f32), pltpu.VMEM((8, LANES), f32),
                            pltpu.VMEM((nbc, LANES), f32), pltpu.VMEM((nbc, LANES), f32),
                            pltpu.VMEM((nbc, LANES), f32), pltpu.VMEM((nblocks, nbc, ATT_W), f32)]),
        out_shape=jax.ShapeDtypeStruct((n_b, 4, ATT_W), f32),
        compiler_params=_cparams("parallel", "arbitrary"),
        name="decode_attn",
    )(page_table, q_s, kn_t, vn_t, ln_t, par, *([k_t] * npg), *([v_t] * npg), *([lf_t] * npg))


def _post_mixer_kernel(*refs, widths):
    n_o = len(widths)
    x_ref = refs[0]
    o_refs = refs[1:1 + n_o]
    w_out_ref, g_ref, wq_ref, mk_ref, mv_ref, wo_ref, y_ref = refs[1 + n_o:]
    x = x_ref[...]
    r0 = 0
    for o_r, w in zip(o_refs, widths):
        x = x + _dot(o_r[...].astype(bf16), w_out_ref[r0:r0 + w, :])
        r0 += w
    h = _rms(x, g_ref[...]).astype(bf16)
    q = _dot(h, wq_ref[...]) * (HEAD_DIM ** -0.5)
    lane = lax.broadcasted_iota(jnp.int32, q.shape, 1)
    mk = mk_ref[...].astype(bf16)
    mv = mv_ref[...].astype(bf16)
    o2 = jnp.zeros(q.shape, f32)
    for hd in range(G_HEADS):
        in_head = (lane >= hd * HEAD_DIM) & (lane < (hd + 1) * HEAD_DIM)
        s = _dot(jnp.where(in_head, q, 0.0).astype(bf16), mk)
        p = jnp.exp(s - jnp.max(s, axis=-1, keepdims=True))
        p = p / jnp.sum(p, axis=-1, keepdims=True)
        o2 = jnp.where(in_head, _dot_nt(p.astype(bf16), mv), o2)
    y_ref[...] = x + _dot(o2.astype(bf16), wo_ref[...])


def _post_mixer(x3, o_list, w_out, g, wq, mk_t, mv_t, wo, tm):
    n_b, t, d = x3.shape
    widths = tuple(o.shape[-1] for o in o_list)
    full = lambda a: pl.BlockSpec(a.shape, lambda b, i: (0,) * a.ndim)
    tok = lambda w: pl.BlockSpec((None, tm, w), lambda b, i: (b, i, 0))
    mem = pl.BlockSpec((None, GROUP_W, N_MEM), lambda b, i: (b, 0, 0))
    return pl.pallas_call(
        functools.partial(_post_mixer_kernel, widths=widths),
        grid=(n_b, t // tm),
        in_specs=[tok(d)] + [tok(w) for w in widths] + [full(w_out), full(g), full(wq), mem, mem, full(wo)],
        out_specs=tok(d),
        out_shape=jax.ShapeDtypeStruct(x3.shape, f32),
        compiler_params=_cparams("parallel", "parallel"),
        name="post_mixer",
    )(x3, *o_list, w_out, g, wq, mk_t, mv_t, wo)


def _ffn_kernel(*refs, moe, final_norm):
    if moe:
        x_ref, g_ref, rt_ref, rb_ref, wg_ref, wu_ref, wd_ref, gf_ref, y_ref, h_s, acc_s, tmp_s, gate_s = refs
    else:
        x_ref, g_ref, wg_ref, wu_ref, wd_ref, gf_ref, y_ref, h_s, acc_s = refs
    e = pl.program_id(1)
    f = pl.program_id(2)
    ne = pl.num_programs(1)
    nf = pl.num_programs(2)

    @pl.when((e == 0) & (f == 0))
    def _():
        h = _rms(x_ref[...], g_ref[...])
        h_s[...] = h.astype(bf16)
        acc_s[...] = jnp.zeros(acc_s.shape, f32)
        if moe:
            logits = _dot(h, rt_ref[...], HI) + rb_ref[...]
            lane = lax.broadcasted_iota(jnp.int32, logits.shape, 1)
            lane_f = lane.astype(f32)
            lg = jnp.where(lane < N_EXPERTS, logits, -3e38)
            v1 = jnp.max(lg, axis=-1, keepdims=True)
            i1 = jnp.min(jnp.where(lg == v1, lane_f, 1e9), axis=-1, keepdims=True)
            lg2 = jnp.where(lane_f == i1, -3e38, lg)
            v2 = jnp.max(lg2, axis=-1, keepdims=True)
            i2 = jnp.min(jnp.where(lg2 == v2, lane_f, 1e9), axis=-1, keepdims=True)
            e2 = jnp.exp(v2 - v1)
            den = 1.0 + e2
            gate_s[...] = jnp.where(lane_f == i1, 1.0 / den, jnp.where(lane_f == i2, e2 / den, 0.0))

    h = h_s[...]
    gg = _dot(h, wg_ref[...])
    uu = _dot(h, wu_ref[...])
    act = (gg * jax.nn.sigmoid(gg) * uu).astype(bf16)
    part = _dot(act, wd_ref[...])
    if moe:
        @pl.when(f == 0)
        def _():
            tmp_s[...] = part

        @pl.when(f > 0)
        def _():
            tmp_s[...] = tmp_s[...] + part

        @pl.when(f == nf - 1)
        def _():
            lane = lax.broadcasted_iota(jnp.int32, gate_s.shape, 1)
            ge = jnp.sum(jnp.where(lane == e, gate_s[...], 0.0), axis=-1, keepdims=True)
            acc_s[...] = acc_s[...] + ge * tmp_s[...]
    else:
        acc_s[...] = acc_s[...] + part

    @pl.when((e == ne - 1) & (f == nf - 1))
    def _():
        y = x_ref[...] + acc_s[...]
        if final_norm:
            y = _rms(y, gf_ref[...])
        y_ref[...] = y


def _ffn(x, g, w_gu, w_d, g_final, *, tm, tf, router=None, final_norm=False):
    n, d = x.shape
    moe = router is not None
    ne = w_gu.shape[0]
    nf = D_FF // tf
    full = lambda a: pl.BlockSpec(a.shape, lambda i, e, f: (0,) * a.ndim)
    in_specs = [pl.BlockSpec((tm, d), lambda i, e, f: (i, 0)), full(g)]
    args = [x, g]
    if moe:
        in_specs += [full(router[0]), full(router[1])]
        args += list(router)
    in_specs += [pl.BlockSpec((None, d, tf), lambda i, e, f: (e, 0, f)),
                 pl.BlockSpec((None, d, tf), lambda i, e, f: (e, 0, nf + f)),
                 pl.BlockSpec((None, tf, d), lambda i, e, f: (e, f, 0)),
                 full(g_final)]
    args += [w_gu, w_gu, w_d, g_final]
    scratch = [pltpu.VMEM((tm, d), bf16), pltpu.VMEM((tm, d), f32)]
    if moe:
        scratch += [pltpu.VMEM((tm, d), f32), pltpu.VMEM((tm, LANES), f32)]
    return pl.pallas_call(
        functools.partial(_ffn_kernel, moe=moe, final_norm=final_norm),
        grid=(n // tm, ne, nf),
        in_specs=in_specs,
        out_specs=pl.BlockSpec((tm, d), lambda i, e, f: (i, 0)),
        out_shape=jax.ShapeDtypeStruct((n, d), f32),
        scratch_shapes=scratch,
        compiler_params=_cparams("parallel", "arbitrary", "arbitrary"),
        name="ffn_moe" if moe else "ffn_dense",
    )(*args)


def _lambda_init(layer):
    return 0.8 - 0.6 * math.exp(-0.3 * layer)


def _row(v, width=LANES, rows=8):
    out = jnp.zeros((rows, width), f32)
    for n, x in enumerate(v):
        x = jnp.asarray(x, f32).reshape(-1)
        out = out.at[n, :x.shape[0]].set(x)
    return out


def kernel(x_prompt, x_sample, cache_k, cache_v, cache_logf, cache_mem_k, cache_mem_v, state_wkv, state_shift,
           page_table, mem_prompt, norm_mix, w_in, b_f, diff_lam, diff_g, rw_mu, rw_w0, rw_w2, rw_a0, rw_a2, rw_g2,
           rw_kk, rw_ka, rw_rk, rw_ln_g, rw_ln_b, w_out, norm_mem, mem_norm, w_mem_q, w_mem_kv, w_mem_o, norm_ffn,
           ffn_gu, ffn_down, moe_router, moe_router_b, moe_gu, moe_down, norm_final):
    n_bp, t_p, d = x_prompt.shape
    n_bs, t_s, _ = x_sample.shape
    depth = w_in.shape[0]
    n_pages = page_table.shape[1]
    assert t_s == 4 and (n_pages * PAGE) % MOBA_BLOCK == 0 and n_pages % PAGES_PER_STEP == 0
    assert t_p % 512 == 0 and d == D_MODEL

    n_pool = cache_k.shape[1]
    k_t = jnp.transpose(cache_k, (0, 1, 3, 4, 2)).reshape(depth, n_pool, ATT_W, PAGE)
    v_t = jnp.transpose(cache_v, (0, 1, 3, 4, 2)).reshape(depth, n_pool, ATT_W, PAGE)
    lf_t = jnp.transpose(cache_logf, (0, 1, 3, 2))
    memk_t = jnp.transpose(cache_mem_k, (0, 1, 3, 4, 2)).reshape(depth, n_bs, GROUP_W, N_MEM)
    memv_t = jnp.transpose(cache_mem_v, (0, 1, 3, 4, 2)).reshape(depth, n_bs, GROUP_W, N_MEM)

    xp = x_prompt.reshape(n_bp * t_p, d)
    xs = x_sample.reshape(n_bs * t_s, d)
    zero_bias = jnp.zeros((1, LANES), f32)
    g_final = norm_final.reshape(1, d)
    outs = {k: [] for k in ("kp", "vp", "lfp", "mkp", "mvp", "wkvp", "shp", "ks", "vs", "lfs", "wkvs", "shs")}

    for l in range(depth):
        lam0 = _lambda_init(l)
        lp = diff_lam[l]
        lam = jnp.exp(jnp.sum(lp[0] * lp[1])) - jnp.exp(jnp.sum(lp[2] * lp[3])) + lam0
        g_pair = jnp.concatenate([diff_g[l], diff_g[l]])
        diff_par = _row([jnp.full((LANES,), lam), g_pair])
        w_l = w_in[l]
        w_cat = jnp.concatenate(
            [w_l[:, 3 * ATT_W + G_HEADS:], w_l[:, :3 * ATT_W],
             jnp.pad(w_l[:, 3 * ATT_W:3 * ATT_W + G_HEADS], ((0, 0), (0, LANES - G_HEADS)))], axis=1).astype(bf16)
        bias_pad = jnp.pad(b_f[l], (0, LANES - G_HEADS)).reshape(1, LANES)
        g_mix = norm_mix[l].reshape(1, d)
        w_out_b = w_out[l].astype(bf16)
        wq_b = w_mem_q[l].astype(bf16)
        wo_b = w_mem_o[l].astype(bf16)
        g_mem = norm_mem[l].reshape(1, d)
        g_ffn = norm_ffn[l].reshape(1, d)
        mu = rw_mu[l].reshape(1, RWKV_W)
        vecs = _row([rw_w0[l], rw_a0[l], rw_kk[l], rw_ka[l], rw_rk[l].reshape(-1)], width=GROUP_W)
        zpad = jnp.zeros((LANES - 64, GROUP_W), f32)
        w2p = jnp.concatenate([rw_w2[l], zpad], axis=0).astype(bf16)
        a2p = jnp.concatenate([zpad, rw_a2[l]], axis=0).astype(bf16)
        g2b = rw_g2[l].astype(bf16)
        ln = _row([rw_ln_g[l], rw_ln_b[l]], width=GROUP_W)
        last = l == depth - 1

        def rw_inputs(proj, n_b, t, shift0, tm):
            pr = proj[:, PR_COL:PR_COL + RWKV_W].reshape(n_b, t, RWKV_W)
            prev = jnp.concatenate([shift0[:, None, :], pr[:, :-1]], axis=1).reshape(n_b * t, RWKV_W)
            return pr, _rw_prep(proj, prev, mu, vecs, w2p, a2p, g2b, tm)

        def ffn(x2, tm):
            if l % 2 == 0:
                i = l // 2
                return _ffn(x2, g_ffn, ffn_gu[i:i + 1].astype(bf16), ffn_down[i:i + 1].astype(bf16), g_final,
                            tm=tm, tf=1408, final_norm=last)
            i = l // 2
            rt = jnp.pad(moe_router[i], ((0, 0), (0, LANES - N_EXPERTS)))
            rb = jnp.pad(moe_router_b[i], (0, LANES - N_EXPERTS)).reshape(1, LANES)
            return _ffn(x2, g_ffn, moe_gu[i].astype(bf16), moe_down[i].astype(bf16), g_final,
                        tm=tm, tf=1408, router=(rt, rb), final_norm=last)

        def wkv_to_bd(s):
            n = s.shape[0]
            m = jnp.zeros((n, GROUP_W, GROUP_W), f32)
            for h in range(G_HEADS):
                m = m.at[:, h * HEAD_DIM:(h + 1) * HEAD_DIM, h * HEAD_DIM:(h + 1) * HEAD_DIM].set(
                    jnp.swapaxes(s[:, h], 1, 2))
            return m

        def bd_to_wkv(m):
            return jnp.stack([jnp.swapaxes(m[:, h * HEAD_DIM:(h + 1) * HEAD_DIM, h * HEAD_DIM:(h + 1) * HEAD_DIM], 1, 2)
                              for h in range(G_HEADS)], axis=1)

        w_qkv_t = jnp.transpose(w_l[:, :3 * ATT_W]).astype(bf16)
        proj, q_t, k_t_out, v_t_out, k_b, v_tb = _norm_matmul(
            xp, g_mix, w_cat, bias_pad, tm=512, chunk=1152, logsig_tail=True, w_t=w_qkv_t, n_b=n_bp)
        pr, rw = rw_inputs(proj, n_bp, t_p, jnp.zeros((n_bp, RWKV_W), f32), 512)
        kaug = _cumsum_logf(proj, n_bp, t_p)
        g_col = jnp.broadcast_to(g_pair[:, None], (LANES, 128))
        o_a = _attn_prompt(k_b, q_t, v_tb, n_bp, t_p, "diff", 0, (diff_par, g_col), lam_scale=1.0 - lam0)
        o_b = _attn_prompt(k_b, q_t, v_tb, n_bp, t_p, "moba", 1)
        o_c = _attn_prompt(k_b, q_t, v_tb, n_bp, t_p, "fox", 2, kaug)
        o_d, m_fin = _rw_chunk(rw, jnp.zeros((n_bp, GROUP_W, GROUP_W), f32), ln, n_bp, t_p)
        kv_mem = _norm_matmul(mem_prompt.reshape(n_bp * N_MEM, d), mem_norm[l].reshape(1, d),
                              w_mem_kv[l].astype(bf16), zero_bias, tm=N_MEM, chunk=2 * GROUP_W, logsig_tail=False)
        mk_p = kv_mem[:, :GROUP_W].reshape(n_bp, N_MEM, GROUP_W)
        mv_p = kv_mem[:, GROUP_W:].reshape(n_bp, N_MEM, GROUP_W)
        to3 = lambda a: a.reshape(n_bp, t_p, a.shape[-1])
        x3 = _post_mixer(to3(xp), [to3(o_a), to3(o_b), to3(o_c), to3(o_d)], w_out_b, g_mem, wq_b,
                         jnp.swapaxes(mk_p, 1, 2), jnp.swapaxes(mv_p, 1, 2), wo_b, 512)
        xp = ffn(x3.reshape(n_bp * t_p, d), 512)
        from_t = lambda a: jnp.transpose(a.reshape(n_bp, 12, HEAD_DIM, t_p), (0, 3, 1, 2))
        outs["kp"].append(from_t(k_t_out))
        outs["vp"].append(from_t(v_t_out))
        outs["lfp"].append(proj[:, FL_COL:FL_COL + G_HEADS].reshape(n_bp, t_p, G_HEADS))
        outs["mkp"].append(mk_p.reshape(n_bp, N_MEM, G_HEADS, HEAD_DIM))
        outs["mvp"].append(mv_p.reshape(n_bp, N_MEM, G_HEADS, HEAD_DIM))
        outs["wkvp"].append(bd_to_wkv(m_fin))
        outs["shp"].append(pr[:, -1])

        proj = _norm_matmul(xs, g_mix, w_cat, bias_pad, tm=n_bs * t_s, chunk=1152, logsig_tail=True)
        pr, rw = rw_inputs(proj, n_bs, t_s, state_shift[l], n_bs * t_s)
        k_new = proj[:, K_COL:K_COL + ATT_W].reshape(n_bs, t_s, ATT_W)
        v_new = proj[:, V_COL:V_COL + ATT_W].reshape(n_bs, t_s, ATT_W)
        lf_new = proj[:, FL_COL:FL_COL + G_HEADS].reshape(n_bs, t_s, G_HEADS)
        pad_t = lambda a: jnp.pad(jnp.swapaxes(a, 1, 2), ((0, 0), (0, 0), (0, PAGE - t_s)))
        o_abc = _decode_attn(page_table, proj[:, Q_COL:Q_COL + ATT_W].reshape(n_bs, t_s, ATT_W), pad_t(k_new), pad_t(v_new),
                             pad_t(lf_new), diff_par, k_t, v_t, lf_t, l, 1.0 - lam0)
        rw_pad = jnp.pad(rw.reshape(n_bs, t_s, 8 * GROUP_W), ((0, 0), (0, RW_CHUNK - t_s), (0, 0)))
        o_d, m_fin = _rw_chunk(rw_pad.reshape(n_bs * RW_CHUNK, 8 * GROUP_W), wkv_to_bd(state_wkv[l]), ln, n_bs, RW_CHUNK)
        o_d = o_d.reshape(n_bs, RW_CHUNK, GROUP_W)[:, :t_s]
        x3 = _post_mixer(xs.reshape(n_bs, t_s, d), [o_abc, o_d], w_out_b, g_mem, wq_b, memk_t[l], memv_t[l], wo_b, t_s)
        xs = ffn(x3.reshape(n_bs * t_s, d), n_bs * t_s)
        outs["ks"].append(k_new.reshape(n_bs, t_s, 12, HEAD_DIM))
        outs["vs"].append(v_new.reshape(n_bs, t_s, 12, HEAD_DIM))
        outs["lfs"].append(lf_new)
        outs["wkvs"].append(bd_to_wkv(m_fin))
        outs["shs"].append(pr[:, -1])

    st = lambda k: jnp.stack(outs[k])
    return (xp.reshape(n_bp, t_p, d), xs.reshape(n_bs, t_s, d), st("kp"), st("vp"), st("lfp"), st("mkp"), st("mvp"),
            st("wkvp"), st("shp"), st("ks"), st("vs"), st("lfs"), st("wkvs"), st("shs"))
```

```python
import functools
import math

import jax
import jax.numpy as jnp
from jax import lax
from jax.experimental import pallas as pl
from jax.experimental.pallas import tpu as pltpu

f32 = jnp.float32
bf16 = jnp.bfloat16

D_MODEL = 1024
HEAD_DIM = 64
G_HEADS = 4
GROUP_W = G_HEADS * HEAD_DIM
ATT_W = 3 * GROUP_W
RWKV_W = 1024
IN_W = 3 * ATT_W + G_HEADS + RWKV_W
PROJ_W = RWKV_W + 3 * ATT_W + 128
PR_COL = 0
Q_COL = RWKV_W
K_COL = Q_COL + ATT_W
V_COL = K_COL + ATT_W
FL_COL = V_COL + ATT_W
MOBA_BLOCK = 256
MOBA_TOPK = 3
PAGE = 128
N_MEM = 256
D_FF = 2816
N_EXPERTS = 8
NORM_EPS = 1e-6
GN_EPS = 64e-5
NEG = -1e30
LANES = 128
RW_CHUNK = 64
VMEM_LIMIT = 56 * 1024 * 1024

HI = lax.Precision.HIGHEST
LOG2E = 1.4426950408889634


def _cparams(*sem):
    return pltpu.CompilerParams(dimension_semantics=sem, vmem_limit_bytes=VMEM_LIMIT)


def _dot(a, b, prec=None):
    return jnp.dot(a, b, preferred_element_type=f32, precision=prec)


def _dot3(a, b):
    a_hi = a.astype(bf16)
    b_hi = b.astype(bf16)
    a_lo = (a - a_hi.astype(f32)).astype(bf16)
    b_lo = (b - b_hi.astype(f32)).astype(bf16)
    return _dot(a_hi, b_hi) + (_dot(a_hi, b_lo) + _dot(a_lo, b_hi))


def _dot_nt(a, b, prec=None):
    return lax.dot_general(a, b, (((1,), (1,)), ((), ())), preferred_element_type=f32, precision=prec)


def _dot_tn(a, b, prec=None):
    return lax.dot_general(a, b, (((0,), (0,)), ((), ())), preferred_element_type=f32, precision=prec)


def _rms(x, g):
    return x * lax.rsqrt(jnp.mean(x * x, axis=-1, keepdims=True) + NORM_EPS) * g


def _pair_rmsnorm(o, g_pair):
    lane = lax.broadcasted_iota(jnp.int32, o.shape, 1)
    lo = lane < HEAD_DIM
    sq = o * o
    s_lo = jnp.sum(jnp.where(lo, sq, 0.0), axis=-1, keepdims=True)
    s_hi = jnp.sum(jnp.where(lo, 0.0, sq), axis=-1, keepdims=True)
    ms = jnp.where(lo, s_lo, s_hi) * (1.0 / HEAD_DIM)
    return o * lax.rsqrt(ms + NORM_EPS) * g_pair


def _norm_matmul_kernel(*refs, chunk, logsig_tail, n_t):
    x_ref, g_ref, w_ref, b_ref = refs[:4]
    wt_ref = refs[4] if n_t else None
    o_ref = refs[4 + bool(n_t)]
    t_refs = refs[5 + bool(n_t):]
    h = _rms(x_ref[...], g_ref[...]).astype(bf16)
    n_out = o_ref.shape[1]
    for c0 in range(0, n_out, chunk):
        c1 = min(c0 + chunk, n_out)
        y = _dot(h, w_ref[:, c0:c1])
        if logsig_tail and c1 == n_out:
            t0 = c1 - c0 - LANES
            tail = jax.nn.log_sigmoid(y[:, t0:] + b_ref[...])
            o_ref[:, c0:c0 + t0] = y[:, :t0]
            o_ref[:, c0 + t0:c1] = tail
        else:
            o_ref[:, c0:c1] = y
    if n_t:
        qt_ref, kt_ref, vt_ref, kb_ref, vtb_ref = t_refs
        qt_ref[...] = _dot_nt(wt_ref[0:ATT_W, :], h)
        kt_ref[...] = _dot_nt(wt_ref[ATT_W:2 * ATT_W, :], h)
        vt = _dot_nt(wt_ref[2 * ATT_W:, :], h)
        vt_ref[...] = vt
        vtb_ref[...] = vt.astype(bf16)
        kb_ref[...] = o_ref[:, K_COL:K_COL + ATT_W].astype(bf16)


def _norm_matmul(x, g, w, bias_pad, *, tm, chunk, logsig_tail, w_t=None, n_b=1):
    n, d = x.shape
    n_out = w.shape[1]
    n_t = 0 if w_t is None else w_t.shape[0] // ATT_W
    in_specs = [pl.BlockSpec((tm, d), lambda i: (i, 0)),
                pl.BlockSpec((1, d), lambda i: (0, 0)),
                pl.BlockSpec((d, n_out), lambda i: (0, 0)),
                pl.BlockSpec((1, LANES), lambda i: (0, 0))]
    args = [x, g, w, bias_pad]
    out_specs = [pl.BlockSpec((tm, n_out), lambda i: (i, 0))]
    out_shape = [jax.ShapeDtypeStruct((n, n_out), f32)]
    if n_t:
        t = n // n_b
        nt = t // tm
        in_specs.append(pl.BlockSpec(w_t.shape, lambda i: (0, 0)))
        args.append(w_t)
        t_spec = pl.BlockSpec((None, ATT_W, tm), lambda i: (i // nt, 0, i % nt))
        out_specs += [t_spec] * 3 + [pl.BlockSpec((tm, ATT_W), lambda i: (i, 0)), t_spec]
        out_shape += ([jax.ShapeDtypeStruct((n_b, ATT_W, t), f32)] * 3
                      + [jax.ShapeDtypeStruct((n, ATT_W), bf16), jax.ShapeDtypeStruct((n_b, ATT_W, t), bf16)])
    res = pl.pallas_call(
        functools.partial(_norm_matmul_kernel, chunk=chunk, logsig_tail=logsig_tail, n_t=n_t),
        grid=(n // tm,),
        in_specs=in_specs,
        out_specs=out_specs,
        out_shape=out_shape,
        compiler_params=_cparams("parallel"),
        name="norm_matmul",
    )(*args)
    return res if n_t else res[0]


def _cumsum_kernel(lf_ref, o_ref, carry_ref, *, tc):
    @pl.when(pl.program_id(1) == 0)
    def _():
        carry_ref[...] = jnp.zeros_like(carry_ref)
    r = lax.broadcasted_iota(jnp.int32, (tc, tc), 0)
    c = lax.broadcasted_iota(jnp.int32, (tc, tc), 1)
    tri = jnp.where(c <= r, 1.0, 0.0).astype(f32)
    cum = _dot(tri, lf_ref[...], HI) + carry_ref[...]
    carry_ref[...] = cum[tc - 1:tc, :]
    neg = cum * (-LOG2E)
    hi = neg.astype(bf16).astype(f32)
    mid = (neg - hi).astype(bf16).astype(f32)
    lo = (neg - hi - mid).astype(bf16).astype(f32)
    src = lax.broadcasted_iota(jnp.int32, (LANES, 2 * LANES), 0)
    dst = lax.broadcasted_iota(jnp.int32, (LANES, 2 * LANES), 1)
    out = jnp.zeros((tc, 2 * LANES), f32)
    for k, piece in enumerate((hi, mid, lo)):
        place = jnp.where((src < G_HEADS) & (dst == LANES * (src // 2) + 3 * (src % 2) + k), 1.0, 0.0)
        out = out + _dot(piece, place, HI)
    o_ref[...] = out.astype(bf16)


def _cumsum_logf(proj, n_b, t):
    tc = 256
    nt = t // tc
    return pl.pallas_call(
        functools.partial(_cumsum_kernel, tc=tc),
        grid=(n_b, nt),
        in_specs=[pl.BlockSpec((tc, LANES), lambda b, i: (b * nt + i, FL_COL // LANES))],
        out_specs=pl.BlockSpec((tc, 2 * LANES), lambda b, i: (b * nt + i, 0)),
        out_shape=jax.ShapeDtypeStruct((n_b * t, 2 * LANES), bf16),
        scratch_shapes=[pltpu.VMEM((1, LANES), f32)],
        compiler_params=_cparams("parallel", "arbitrary"),
        name="cumsum_logf",
    )(proj)


def _attn_kernel(*refs, mode, tq, tk, lam_scale):
    if mode == "diff":
        qt_ref, k_ref, vt_ref, par_ref, g_ref, o_ref = refs[:6]
    elif mode == "fox":
        qt_ref, k_ref, vt_ref, kaug_ref, o_ref = refs[:5]
    else:
        qt_ref, k_ref, vt_ref, o_ref = refs[:4]
    n_in = {"diff": 6, "fox": 5, "moba": 4}[mode]
    m_s, l_s, acc_s, sc_s = (refs[n_in:n_in + 2], refs[n_in + 2:n_in + 4], refs[n_in + 4:n_in + 8],
                             refs[n_in + 8:n_in + 10])
    if mode == "moba":
        kmean_s, sel_s = refs[n_in + 10], refs[n_in + 11:n_in + 13]
    i = pl.program_id(1)
    nvar = 4 if mode == "diff" else 2
    cols = nvar * tq
    drow = lax.broadcasted_iota(jnp.int32, (LANES, tq), 0)
    qpos = i * tq + (lax.broadcasted_iota(jnp.int32, (1, cols), 1) & (tq - 1))
    qs = []
    qs32 = []
    for hp in range(2):
        qt = qt_ref[hp * LANES:(hp + 1) * LANES, :]
        if mode == "diff":
            half = HEAD_DIM // 2
            scale = half ** -0.5 * LOG2E
            parts = [jnp.where((drow >= v * half) & (drow < (v + 1) * half), qt, 0.0) for v in range(4)]
        else:
            scale = HEAD_DIM ** -0.5 * LOG2E
            parts = [jnp.where(drow < HEAD_DIM, qt, 0.0), jnp.where(drow >= HEAD_DIM, qt, 0.0)]
        q32 = jnp.concatenate(parts, axis=1) * scale
        qs32.append(q32)
        if mode == "fox":
            r = lax.broadcasted_iota(jnp.int32, (LANES, cols), 0)
            c = lax.broadcasted_iota(jnp.int32, (LANES, cols), 1)
            ind = jnp.where(((r < 3) & (c < tq)) | ((r >= 3) & (r < 6) & (c >= tq)), 1.0, 0.0)
            qs.append(jnp.concatenate([q32, ind], axis=0).astype(bf16))
        else:
            qs.append(q32.astype(bf16))
        m_s[hp][...] = jnp.full(m_s[hp].shape, NEG, f32)
        l_s[hp][...] = jnp.zeros(l_s[hp].shape, f32)
        for hh in range(2):
            acc_s[2 * hp + hh][...] = jnp.zeros(acc_s[2 * hp + hh].shape, f32)

    if mode == "moba":
        nb = k_ref.shape[0] // MOBA_BLOCK

        @pl.when(i == 0)
        def _():
            kmean_s[...] = jnp.zeros(kmean_s.shape, f32)

            def blk(n, _):
                kb = k_ref[pl.ds(pl.multiple_of(n * MOBA_BLOCK, MOBA_BLOCK), MOBA_BLOCK), :]
                kmean_s[pl.ds(n, 1), :] = jnp.mean(kb.astype(f32), axis=0, keepdims=True)
                return 0
            lax.fori_loop(0, nb, blk, 0)

        nidx = lax.broadcasted_iota(jnp.int32, (LANES, cols), 0)
        nidx_f = nidx.astype(f32)
        own = qpos // MOBA_BLOCK
        own_f = own.astype(f32)
        for hp in range(2):
            gate = _dot(kmean_s[:, hp * LANES:(hp + 1) * LANES], qs32[hp], HI)
            g = jnp.where(nidx < own, gate, NEG)
            sel = jnp.zeros((LANES, cols), f32)
            for _ in range(MOBA_TOPK):
                mx = jnp.max(g, axis=0, keepdims=True)
                first = jnp.min(jnp.where(g == mx, nidx_f, 1e9), axis=0, keepdims=True)
                pick = nidx_f == first
                sel = jnp.where(pick & (first < own_f), 1.0, sel)
                g = jnp.where(pick, -3e38, g)
            sel_s[hp][...] = sel

    def scores(j, hp):
        start = pl.multiple_of(j * tk, tk)
        lanes = slice(hp * LANES, (hp + 1) * LANES)
        kb = k_ref[pl.ds(start, tk), lanes]
        if mode == "fox":
            kb = jnp.concatenate([kb, kaug_ref[pl.ds(start, tk), lanes]], axis=1)
        return _dot(kb, qs[hp])

    def absorb(s, j, hp, diagonal):
        start = pl.multiple_of(j * tk, tk)
        valid = None
        if mode == "moba" and not diagonal:
            valid = sel_s[hp][pl.ds(j, 1), :] > 0.5
        if diagonal:
            kpos = start + lax.broadcasted_iota(jnp.int32, (tk, 1), 0)
            valid = kpos <= qpos
        if valid is not None:
            s = jnp.where(valid, s, NEG)
        m_old = m_s[hp][...]
        m_new = jnp.maximum(m_old, jnp.max(s, axis=0, keepdims=True))
        alpha = jnp.exp2(m_old - m_new)
        p = jnp.exp2(s - m_new)
        if valid is not None:
            p = jnp.where(valid, p, 0.0)
        l_s[hp][...] = alpha * l_s[hp][...] + jnp.sum(p, axis=0, keepdims=True)
        pb = p.astype(bf16)
        hc = cols // 2
        for hh in range(2):
            vt = vt_ref[hp * LANES + hh * HEAD_DIM:hp * LANES + (hh + 1) * HEAD_DIM, pl.ds(start, tk)]
            acc = acc_s[2 * hp + hh]
            acc[...] = alpha[:, hh * hc:(hh + 1) * hc] * acc[...] + _dot(vt, pb[:, hh * hc:(hh + 1) * hc])
        m_s[hp][...] = m_new

    n_full = (i * tq) // tk
    for hp in range(2):
        sc_s[hp][...] = scores(0, hp)

    def body(j, _):
        for hp in range(2):
            nxt = scores(j + 1, hp)
            absorb(sc_s[hp][...], j, hp, False)
            sc_s[hp][...] = nxt
        return 0
    lax.fori_loop(0, n_full, body, 0)
    for hp in range(2):
        absorb(sc_s[hp][...], n_full, hp, True)

    hc = cols // 2
    for hp in range(2):
        inv_l = 1.0 / l_s[hp][...]
        a = [acc_s[2 * hp + hh][...] * inv_l[:, hh * hc:(hh + 1) * hc] for hh in range(2)]
        if mode == "diff":
            lam = par_ref[0:1, 0:1]
            o = jnp.concatenate([a_h[:, 0:tq] - lam * a_h[:, tq:2 * tq] for a_h in a], axis=0)
            lo = drow < HEAD_DIM
            sq = o * o
            s_lo = jnp.sum(jnp.where(lo, sq, 0.0), axis=0, keepdims=True)
            s_hi = jnp.sum(jnp.where(lo, 0.0, sq), axis=0, keepdims=True)
            ms = jnp.where(lo, s_lo, s_hi) * (1.0 / HEAD_DIM)
            o = o * lax.rsqrt(ms + NORM_EPS) * g_ref[...] * lam_scale
        else:
            o = jnp.concatenate(a, axis=0)
        o_ref[:, hp * LANES:(hp + 1) * LANES] = o.T


def _attn_prompt(k_b, q_t, v_t, n_b, t, mode, group, extra=None, lam_scale=1.0):
    tk = MOBA_BLOCK if mode == "moba" else 512
    tq = 128 if mode == "diff" else 256
    nq = t // tq
    nvar = 4 if mode == "diff" else 2
    cols = nvar * tq
    in_specs = [pl.BlockSpec((None, GROUP_W, tq), lambda b, i: (b, group, i)),
                pl.BlockSpec((t, GROUP_W), lambda b, i: (b, group)),
                pl.BlockSpec((None, GROUP_W, t), lambda b, i: (b, group, 0))]
    args = [q_t, k_b, v_t]
    scratch = ([pltpu.VMEM((1, cols), f32)] * 4 + [pltpu.VMEM((HEAD_DIM, cols // 2), f32)] * 4
               + [pltpu.VMEM((tk, cols), f32)] * 2)
    if mode == "diff":
        in_specs += [pl.BlockSpec((8, LANES), lambda b, i: (0, 0)),
                     pl.BlockSpec((LANES, tq), lambda b, i: (0, 0))]
        args += list(extra)
    elif mode == "fox":
        in_specs.append(pl.BlockSpec((t, GROUP_W), lambda b, i: (b, 0)))
        args.append(extra)
    else:
        scratch += [pltpu.VMEM((LANES, GROUP_W), f32)] + [pltpu.VMEM((LANES, cols), f32)] * 2
    return pl.pallas_call(
        functools.partial(_attn_kernel, mode=mode, tq=tq, tk=tk, lam_scale=lam_scale),
        grid=(n_b, nq),
        in_specs=in_specs,
        out_specs=pl.BlockSpec((tq, GROUP_W), lambda b, i: (b * nq + i, 0)),
        out_shape=jax.ShapeDtypeStruct((n_b * t, GROUP_W), f32),
        scratch_shapes=scratch,
        compiler_params=_cparams("parallel", "arbitrary"),
        name="attn_" + mode,
    )(*args)


def _head_ones(n):
    r = lax.broadcasted_iota(jnp.int32, (n, n), 0) // HEAD_DIM
    c = lax.broadcasted_iota(jnp.int32, (n, n), 1) // HEAD_DIM
    return jnp.where(r == c, 1.0, 0.0).astype(f32)


def _rw_prep_kernel(pr_ref, prev_ref, mu_ref, vec_ref, w2_ref, a2_ref, g2_ref, o_ref):
    pr = pr_ref[...]
    xm = pr + (prev_ref[...] - pr) * mu_ref[...]
    r = xm[:, 0:GROUP_W]
    k = xm[:, GROUP_W:2 * GROUP_W]
    v = xm[:, 2 * GROUP_W:3 * GROUP_W]
    x_lora = xm[:, 3 * GROUP_W:3 * GROUP_W + LANES]
    x_gate = xm[:, 3 * GROUP_W + LANES:]
    w0, a0, kk_p, ka, rk = (vec_ref[n:n + 1, :] for n in range(5))
    w_lin = w0 + _dot(jnp.tanh(x_lora).astype(bf16), w2_ref[...])
    w_log = -jax.nn.softplus(-w_lin) - 0.5
    lw = -jnp.exp(w_log)
    a = jax.nn.sigmoid(a0 + _dot(x_lora.astype(bf16), a2_ref[...]))
    g = _dot(jax.nn.sigmoid(x_gate).astype(bf16), g2_ref[...])
    ones = _head_ones(GROUP_W)
    kk = k * kk_p
    kk = kk * lax.rsqrt(_dot(kk * kk, ones, HI) + 1e-12)
    kh = k * (1.0 + (a - 1.0) * ka)
    bonus = _dot(r * kh * rk, ones, HI) * v
    for n, val in enumerate((r, lw, kh, v, kk, kk * a, g, bonus)):
        o_ref[:, n * GROUP_W:(n + 1) * GROUP_W] = val


def _rw_prep(proj, prev, mu, vecs, w2p, a2p, g2, tm):
    n = proj.shape[0]
    full = lambda shape: pl.BlockSpec(shape, lambda i: (0,) * len(shape))
    return pl.pallas_call(
        _rw_prep_kernel,
        grid=(n // tm,),
        in_specs=[pl.BlockSpec((tm, RWKV_W), lambda i: (i, PR_COL // RWKV_W)),
                  pl.BlockSpec((tm, RWKV_W), lambda i: (i, 0)),
                  full((1, RWKV_W)), full((8, GROUP_W)), full((LANES, GROUP_W)), full((LANES, GROUP_W)),
                  full((LANES, GROUP_W))],
        out_specs=pl.BlockSpec((tm, 8 * GROUP_W), lambda i: (i, 0)),
        out_shape=jax.ShapeDtypeStruct((n, 8 * GROUP_W), f32),
        compiler_params=_cparams("parallel"),
        name="rw_prep",
    )(proj, prev, mu, vecs, w2p, a2p, g2)


RW_STEP = 2


def _rw_chunk_kernel(x_ref, m0_ref, ln_ref, o_ref, mfin_ref, m_s):
    c = pl.program_id(1)
    nc = pl.num_programs(1)
    C = RW_CHUNK
    W = GROUP_W

    @pl.when(c == 0)
    def _():
        m_s[...] = m0_ref[...]

    tr = lax.broadcasted_iota(jnp.int32, (C, C), 0)
    tc = lax.broadcasted_iota(jnp.int32, (C, C), 1)
    tri = jnp.where(tc <= tr, 1.0, 0.0).astype(f32)
    ri = lax.broadcasted_iota(jnp.int32, (W, W), 0)
    ci = lax.broadcasted_iota(jnp.int32, (W, W), 1)
    same = (ri // C) == (ci // HEAD_DIM)
    same_t = (ri // C) == (ci // C)
    strict = same_t & ((ci % C) < (ri % C))
    incl = same_t & ((ci % C) <= (ri % C))
    eye = jnp.where(ri == ci, 1.0, 0.0).astype(f32)
    mean_mat = _head_ones(W) * (1.0 / HEAD_DIM)

    def sm(x):
        return jnp.where(same, jnp.concatenate([x] * G_HEADS, axis=0), 0.0)

    def mm(x, y):
        return _dot(x.astype(bf16), y.astype(bf16))

    def chunk_ops(u):
        r, lw, kh, v, kk, b = (x_ref[u * C:(u + 1) * C, n * W:(n + 1) * W] for n in range(6))
        cum = _dot(tri, lw, HI)
        inv = jnp.exp(-cum)
        g_end = jnp.exp(cum[C - 1:C, :])
        a_sm = sm(-kk * jnp.exp(cum - lw))
        r_sm = sm(r * jnp.exp(cum))
        b_sm = sm(b * inv)
        k_sm = sm(kh * inv)
        v_sm = sm(v)
        gram = _dot_nt(jnp.concatenate([a_sm, r_sm], axis=0).astype(bf16),
                       jnp.concatenate([b_sm, k_sm], axis=0).astype(bf16))
        n_ab = jnp.where(strict, gram[:W, :W], 0.0)
        a_ak = jnp.where(strict, gram[:W, W:], 0.0)
        a_rb = jnp.where(incl, gram[W:, :W], 0.0)
        a_rk = jnp.where(incl, gram[W:, W:], 0.0)
        t_inv = eye + n_ab
        n_pow = n_ab
        for _ in range(int(math.log2(C)) - 1):
            n_pow = mm(n_pow, n_pow)
            t_inv = t_inv + mm(n_pow, t_inv)
        akv = mm(a_ak, v_sm)
        tx = mm(t_inv, jnp.concatenate([a_sm, akv], axis=1))
        y = mm(a_rb, tx)
        w_o = r_sm + y[:, :W]
        o_0 = y[:, W:] + mm(a_rk, v_sm)
        bt = (b_sm * g_end).astype(bf16)
        kt = (k_sm * g_end).astype(bf16)
        pq = _dot_tn(bt, tx.astype(bf16))
        p_mat = jnp.where(ri == ci, g_end, 0.0) + pq[:, :W]
        q_mat = pq[:, W:] + _dot_tn(kt, v_sm.astype(bf16))
        return w_o, o_0, p_mat, q_mat

    ops = [chunk_ops(u) for u in range(RW_STEP)]
    m = m_s[...]
    for u, (w_o, o_0, p_mat, q_mat) in enumerate(ops):
        o_sm = _dot3(w_o, m) + o_0
        m = _dot3(p_mat, m) + q_mat
        o = o_sm[0:C] + o_sm[C:2 * C] + o_sm[2 * C:3 * C] + o_sm[3 * C:4 * C]
        mu = _dot(o, mean_mat, HI)
        d = o - mu
        var = _dot(d * d, mean_mat, HI)
        o_n = d * lax.rsqrt(var + GN_EPS) * ln_ref[0:1, :] + ln_ref[1:2, :]
        g = x_ref[u * C:(u + 1) * C, 6 * W:7 * W]
        bonus = x_ref[u * C:(u + 1) * C, 7 * W:8 * W]
        o_ref[u * C:(u + 1) * C, :] = (o_n + bonus) * g
    m_s[...] = m

    @pl.when(c == nc - 1)
    def _():
        mfin_ref[...] = m


def _rw_chunk(rw, m0, ln, n_b, t):
    rows = RW_STEP * RW_CHUNK
    nc = t // rows
    return pl.pallas_call(
        _rw_chunk_kernel,
        grid=(n_b, nc),
        in_specs=[pl.BlockSpec((rows, 8 * GROUP_W), lambda b, c: (b * nc + c, 0)),
                  pl.BlockSpec((None, GROUP_W, GROUP_W), lambda b, c: (b, 0, 0)),
                  pl.BlockSpec((8, GROUP_W), lambda b, c: (0, 0))],
        out_specs=[pl.BlockSpec((rows, GROUP_W), lambda b, c: (b * nc + c, 0)),
                   pl.BlockSpec((None, GROUP_W, GROUP_W), lambda b, c: (b, 0, 0))],
        out_shape=[jax.ShapeDtypeStruct((n_b * t, GROUP_W), f32),
                   jax.ShapeDtypeStruct((n_b, GROUP_W, GROUP_W), f32)],
        scratch_shapes=[pltpu.VMEM((GROUP_W, GROUP_W), f32)],
        compiler_params=_cparams("parallel", "arbitrary"),
        name="rw_chunk",
    )(rw, m0, ln)


N_COLS = 64
COL_B, COL_C = 32, 48
PAGES_PER_STEP = 8


def _col_maps(shape, axis):
    c = lax.broadcasted_iota(jnp.int32, shape, axis)
    tok = c % 4
    head = jnp.where(c < COL_B, c // 8, jnp.where(c < COL_C, 4 + (c - COL_B) // 4, 8 + (c - COL_C) // 4))
    half = (c // 4) % 2
    lo = head * HEAD_DIM + jnp.where(c < COL_B, half * (HEAD_DIM // 2), 0)
    hi = jnp.where(c < COL_B, lo + HEAD_DIM // 2, lo + HEAD_DIM)
    return c, tok, head, lo, hi


def _decode_kernel(*refs, lam_scale, nj):
    npg = PAGES_PER_STEP
    pt_ref, q_ref, kn_ref, vn_ref, ln_ref, par_ref = refs[:6]
    k_refs, v_refs, l_refs = refs[6:6 + npg], refs[6 + npg:6 + 2 * npg], refs[6 + 2 * npg:6 + 3 * npg]
    o_ref, qbd_s, m_s, l_s, acc_s, carry_s, bm_s, bl_s, bg_s, bacc_s = refs[6 + 3 * npg:]
    j = pl.program_id(1)
    nblk = npg // 2
    c1, tok1, head1, lo1, hi1 = _col_maps((N_COLS, 1), 0)

    @pl.when(j == 0)
    def _():
        q4 = q_ref[...]
        lanes = lax.broadcasted_iota(jnp.int32, (N_COLS, ATT_W), 1)
        qb = jnp.zeros((N_COLS, ATT_W), f32)
        for t in range(4):
            qb = jnp.where(tok1 == t, q4[t:t + 1, :], qb)
        scale = jnp.where(c1 < COL_B, (HEAD_DIM // 2) ** -0.5, HEAD_DIM ** -0.5)
        qbd_s[...] = (jnp.where((lanes >= lo1) & (lanes < hi1), qb, 0.0) * scale).astype(bf16)
        m_s[...] = jnp.full(m_s.shape, NEG, f32)
        l_s[...] = jnp.zeros(l_s.shape, f32)
        acc_s[...] = jnp.zeros(acc_s.shape, f32)
        carry_s[...] = jnp.zeros(carry_s.shape, f32)
        bm_s[...] = jnp.zeros(bm_s.shape, f32)
        bl_s[...] = jnp.zeros(bl_s.shape, f32)
        bg_s[...] = jnp.zeros(bg_s.shape, f32)

    qbd = qbd_s[...]
    is_c = c1 >= COL_C

    def forget_bias(cum):
        bias = jnp.zeros((N_COLS, cum.shape[1]), f32)
        for h in range(G_HEADS):
            bias = jnp.where(is_c & (head1 == 8 + h), -cum[h:h + 1, :], bias)
        return bias

    def upper(n):
        r = lax.broadcasted_iota(jnp.int32, (n, n), 0)
        c = lax.broadcasted_iota(jnp.int32, (n, n), 1)
        return jnp.where(r <= c, 1.0, 0.0).astype(f32)

    def merge(parts):
        m_old = m_s[...]
        m_new = m_old
        for m_k, _, _ in parts:
            m_new = jnp.maximum(m_new, m_k)
        a = jnp.exp(m_old - m_new)
        l_new = a * l_s[...]
        acc_new = a * acc_s[...]
        for m_k, l_k, acc_k in parts:
            e_k = jnp.exp(m_k - m_new)
            l_new = l_new + e_k * l_k
            acc_new = acc_new + e_k * acc_k
        l_s[...] = l_new
        acc_s[...] = acc_new
        m_s[...] = m_new

    kw = 2 * PAGE
    lf = jnp.concatenate([jnp.concatenate([l_refs[2 * k][...], l_refs[2 * k + 1][...]], axis=1) for k in range(nblk)],
                         axis=0)
    cum_loc = _dot(lf, upper(kw), HI)
    carry = carry_s[0:G_HEADS, 0:1]
    blane = lax.broadcasted_iota(jnp.int32, (COL_C - COL_B, LANES), 1)
    parts = []
    for k in range(nblk):
        cum = cum_loc[G_HEADS * k:G_HEADS * (k + 1), :] + carry
        carry = cum[:, kw - 1:kw]
        s = jnp.concatenate([_dot(qbd, k_refs[2 * k][...].astype(bf16)),
                             _dot(qbd, k_refs[2 * k + 1][...].astype(bf16))], axis=1)
        gate = jnp.sum(s[COL_B:COL_C], axis=-1, keepdims=True) * (1.0 / MOBA_BLOCK)
        s = s + forget_bias(cum)
        m_blk = jnp.max(s, axis=-1, keepdims=True)
        p = jnp.exp(s - m_blk)
        l_blk = jnp.sum(p, axis=-1, keepdims=True)
        pb = p.astype(bf16)
        acc_blk = (_dot_nt(pb[:, :PAGE], v_refs[2 * k][...].astype(bf16))
                   + _dot_nt(pb[:, PAGE:], v_refs[2 * k + 1][...].astype(bf16)))
        parts.append((m_blk, l_blk, acc_blk))
        n = j * nblk + k
        bm_s[...] = jnp.where(blane == n, m_blk[COL_B:COL_C], bm_s[...])
        bl_s[...] = jnp.where(blane == n, l_blk[COL_B:COL_C], bl_s[...])
        bg_s[...] = jnp.where(blane == n, gate, bg_s[...])
        bacc_s[n] = acc_blk[COL_B:COL_C]
    carry_s[0:G_HEADS, :] = jnp.broadcast_to(carry, (G_HEADS, LANES))
    merge(parts)

    @pl.when(j == nj - 1)
    def _():
        sn = _dot(qbd, kn_ref[...].astype(bf16))
        cum_n = _dot(ln_ref[...], upper(PAGE), HI) + carry_s[0:G_HEADS, 0:1]
        klane = lax.broadcasted_iota(jnp.int32, (N_COLS, PAGE), 1)
        sn = jnp.where(klane <= tok1, sn + forget_bias(cum_n), NEG)
        m_n = jnp.max(sn, axis=-1, keepdims=True)
        p_n = jnp.exp(sn - m_n)
        l_n = jnp.sum(p_n, axis=-1, keepdims=True)
        acc_n = _dot_nt(p_n.astype(bf16), vn_ref[...].astype(bf16))
        merge([(m_n, l_n, acc_n)])

        nb = nj * nblk
        bgate = jnp.where(blane < nb, bg_s[...], NEG)
        blane_f = blane.astype(f32)
        sel = jnp.zeros(bgate.shape, f32)
        for _ in range(MOBA_TOPK):
            mx = jnp.max(bgate, axis=-1, keepdims=True)
            first = jnp.min(jnp.where(bgate == mx, blane_f, 1e9), axis=-1, keepdims=True)
            pick = blane_f == first
            sel = jnp.where(pick & (first < nb), 1.0, sel)
            bgate = jnp.where(pick, -3e38, bgate)
        on = sel > 0.5
        mb_n, lb_n, accb_n = m_n[COL_B:COL_C], l_n[COL_B:COL_C], acc_n[COL_B:COL_C]
        m_tot = jnp.maximum(jnp.max(jnp.where(on, bm_s[...], NEG), axis=-1, keepdims=True), mb_n)
        wgt = jnp.where(on, jnp.exp(bm_s[...] - m_tot), 0.0)
        e_n = jnp.exp(mb_n - m_tot)
        l_b = jnp.sum(wgt * bl_s[...], axis=-1, keepdims=True) + e_n * lb_n

        def add_block(n, acc):
            w_n = jnp.sum(jnp.where(blane == n, wgt, 0.0), axis=-1, keepdims=True)
            return acc + w_n * bacc_s[n]
        acc_b = lax.fori_loop(0, nb, add_block, e_n * accb_n)
        l_s[COL_B:COL_C, :] = l_b
        acc_s[COL_B:COL_C, :] = acc_b

        lam = par_ref[0:1, 0:1]
        coef = jnp.where((c1 < COL_B) & ((c1 // 4) % 2 == 1), -lam, 1.0) / l_s[...]
        lanes = lax.broadcasted_iota(jnp.int32, (N_COLS, ATT_W), 1)
        keep = (lanes >= head1 * HEAD_DIM) & (lanes < (head1 + 1) * HEAD_DIM)
        contrib = jnp.where(keep, acc_s[...] * coef, 0.0)
        _, tok_r, _, _, _ = _col_maps((8, N_COLS), 1)
        trow = lax.broadcasted_iota(jnp.int32, (8, N_COLS), 0)
        o_tok = _dot(jnp.where(tok_r == trow, 1.0, 0.0).astype(f32), contrib, HI)[0:4]
        o_ref[:, 2 * LANES:] = o_tok[:, 2 * LANES:]
        for hp in range(2):
            o_ref[:, hp * LANES:(hp + 1) * LANES] = (
                _pair_rmsnorm(o_tok[:, hp * LANES:(hp + 1) * LANES], par_ref[1:2, :]) * lam_scale)


def _decode_attn(page_table, q_s, kn_t, vn_t, ln_t, par, k_t, v_t, lf_t, layer, lam_scale):
    n_b, n_pages = page_table.shape
    npg = PAGES_PER_STEP
    nj = n_pages // npg
    nblocks = n_pages // 2

    def page_spec(rows, g):
        return pl.BlockSpec((None, None, rows, PAGE), lambda b, j, pt: (layer, pt[b, npg * j + g], 0, 0))
    new_spec = lambda rows: pl.BlockSpec((None, rows, PAGE), lambda b, j, pt: (b, 0, 0))
    nbc = COL_C - COL_B
    return pl.pallas_call(
        functools.partial(_decode_kernel, lam_scale=lam_scale, nj=nj),
        grid_spec=pltpu.PrefetchScalarGridSpec(
            num_scalar_prefetch=1,
            grid=(n_b, nj),
            in_specs=([pl.BlockSpec((None, 4, ATT_W), lambda b, j, pt: (b, 0, 0)),
                       new_spec(ATT_W), new_spec(ATT_W), new_spec(G_HEADS),
                       pl.BlockSpec((8, LANES), lambda b, j, pt: (0, 0))]
                      + [page_spec(ATT_W, g) for g in range(npg)] * 2
                      + [page_spec(G_HEADS, g) for g in range(npg)]),
            out_specs=pl.BlockSpec((None, 4, ATT_W), lambda b, j, pt: (b, 0, 0)),
            scratch_shapes=[pltpu.VMEM((N_COLS, ATT_W), bf16),
                            pltpu.VMEM((N_COLS, 1), f32), pltpu.VMEM((N_COLS, 1), f32),
                            pltpu.VMEM((N_COLS, ATT_W), f32), pltpu.VMEM((8, LANES), f32),
                            pltpu.VMEM((nbc, LANES), f32), pltpu.VMEM((nbc, LANES), f32),
                            pltpu.VMEM((nbc, LANES), f32), pltpu.VMEM((nblocks, nbc, ATT_W), f32)]),
        out_shape=jax.ShapeDtypeStruct((n_b, 4, ATT_W), f32),
        compiler_params=_cparams("parallel", "arbitrary"),
        name="decode_attn",
    )(page_table, q_s, kn_t, vn_t, ln_t, par, *([k_t] * npg), *([v_t] * npg), *([lf_t] * npg))


def _post_mixer_kernel(*refs, widths):
    n_o = len(widths)
    x_ref = refs[0]
    o_refs = refs[1:1 + n_o]
    w_out_ref, g_ref, wq_ref, mk_ref, mv_ref, wo_ref, y_ref = refs[1 + n_o:]
    x = x_ref[...]
    r0 = 0
    for o_r, w in zip(o_refs, widths):
        x = x + _dot(o_r[...].astype(bf16), w_out_ref[r0:r0 + w, :])
        r0 += w
    h = _rms(x, g_ref[...]).astype(bf16)
    q = _dot(h, wq_ref[...]) * (HEAD_DIM ** -0.5)
    lane = lax.broadcasted_iota(jnp.int32, q.shape, 1)
    mk = mk_ref[...].astype(bf16)
    mv = mv_ref[...].astype(bf16)
    o2 = jnp.zeros(q.shape, f32)
    for hd in range(G_HEADS):
        in_head = (lane >= hd * HEAD_DIM) & (lane < (hd + 1) * HEAD_DIM)
        s = _dot(jnp.where(in_head, q, 0.0).astype(bf16), mk)
        p = jnp.exp(s - jnp.max(s, axis=-1, keepdims=True))
        p = p / jnp.sum(p, axis=-1, keepdims=True)
        o2 = jnp.where(in_head, _dot_nt(p.astype(bf16), mv), o2)
    y_ref[...] = x + _dot(o2.astype(bf16), wo_ref[...])


def _post_mixer(x3, o_list, w_out, g, wq, mk_t, mv_t, wo, tm):
    n_b, t, d = x3.shape
    widths = tuple(o.shape[-1] for o in o_list)
    full = lambda a: pl.BlockSpec(a.shape, lambda b, i: (0,) * a.ndim)
    tok = lambda w: pl.BlockSpec((None, tm, w), lambda b, i: (b, i, 0))
    mem = pl.BlockSpec((None, GROUP_W, N_MEM), lambda b, i: (b, 0, 0))
    return pl.pallas_call(
        functools.partial(_post_mixer_kernel, widths=widths),
        grid=(n_b, t // tm),
        in_specs=[tok(d)] + [tok(w) for w in widths] + [full(w_out), full(g), full(wq), mem, mem, full(wo)],
        out_specs=tok(d),
        out_shape=jax.ShapeDtypeStruct(x3.shape, f32),
        compiler_params=_cparams("parallel", "parallel"),
        name="post_mixer",
    )(x3, *o_list, w_out, g, wq, mk_t, mv_t, wo)


def _ffn_kernel(*refs, moe, final_norm):
    if moe:
        x_ref, g_ref, rt_ref, rb_ref, wg_ref, wu_ref, wd_ref, gf_ref, y_ref, h_s, acc_s, tmp_s, gate_s = refs
    else:
        x_ref, g_ref, wg_ref, wu_ref, wd_ref, gf_ref, y_ref, h_s, acc_s = refs
    e = pl.program_id(1)
    f = pl.program_id(2)
    ne = pl.num_programs(1)
    nf = pl.num_programs(2)

    @pl.when((e == 0) & (f == 0))
    def _():
        h = _rms(x_ref[...], g_ref[...])
        h_s[...] = h.astype(bf16)
        acc_s[...] = jnp.zeros(acc_s.shape, f32)
        if moe:
            logits = _dot(h, rt_ref[...], HI) + rb_ref[...]
            lane = lax.broadcasted_iota(jnp.int32, logits.shape, 1)
            lane_f = lane.astype(f32)
            lg = jnp.where(lane < N_EXPERTS, logits, -3e38)
            v1 = jnp.max(lg, axis=-1, keepdims=True)
            i1 = jnp.min(jnp.where(lg == v1, lane_f, 1e9), axis=-1, keepdims=True)
            lg2 = jnp.where(lane_f == i1, -3e38, lg)
            v2 = jnp.max(lg2, axis=-1, keepdims=True)
            i2 = jnp.min(jnp.where(lg2 == v2, lane_f, 1e9), axis=-1, keepdims=True)
            e2 = jnp.exp(v2 - v1)
            den = 1.0 + e2
            gate_s[...] = jnp.where(lane_f == i1, 1.0 / den, jnp.where(lane_f == i2, e2 / den, 0.0))

    h = h_s[...]
    gg = _dot(h, wg_ref[...])
    uu = _dot(h, wu_ref[...])
    act = (gg * jax.nn.sigmoid(gg) * uu).astype(bf16)
    part = _dot(act, wd_ref[...])
    if moe:
        @pl.when(f == 0)
        def _():
            tmp_s[...] = part

        @pl.when(f > 0)
        def _():
            tmp_s[...] = tmp_s[...] + part

        @pl.when(f == nf - 1)
        def _():
            lane = lax.broadcasted_iota(jnp.int32, gate_s.shape, 1)
            ge = jnp.sum(jnp.where(lane == e, gate_s[...], 0.0), axis=-1, keepdims=True)
            acc_s[...] = acc_s[...] + ge * tmp_s[...]
    else:
        acc_s[...] = acc_s[...] + part

    @pl.when((e == ne - 1) & (f == nf - 1))
    def _():
        y = x_ref[...] + acc_s[...]
        if final_norm:
            y = _rms(y, gf_ref[...])
        y_ref[...] = y


def _ffn(x, g, w_gu, w_d, g_final, *, tm, tf, router=None, final_norm=False):
    n, d = x.shape
    moe = router is not None
    ne = w_gu.shape[0]
    nf = D_FF // tf
    full = lambda a: pl.BlockSpec(a.shape, lambda i, e, f: (0,) * a.ndim)
    in_specs = [pl.BlockSpec((tm, d), lambda i, e, f: (i, 0)), full(g)]
    args = [x, g]
    if moe:
        in_specs += [full(router[0]), full(router[1])]
        args += list(router)
    in_specs += [pl.BlockSpec((None, d, tf), lambda i, e, f: (e, 0, f)),
                 pl.BlockSpec((None, d, tf), lambda i, e, f: (e, 0, nf + f)),
                 pl.BlockSpec((None, tf, d), lambda i, e, f: (e, f, 0)),
                 full(g_final)]
    args += [w_gu, w_gu, w_d, g_final]
    scratch = [pltpu.VMEM((tm, d), bf16), pltpu.VMEM((tm, d), f32)]
    if moe:
        scratch += [pltpu.VMEM((tm, d), f32), pltpu.VMEM((tm, LANES), f32)]
    return pl.pallas_call(
        functools.partial(_ffn_kernel, moe=moe, final_norm=final_norm),
        grid=(n // tm, ne, nf),
        in_specs=in_specs,
        out_specs=pl.BlockSpec((tm, d), lambda i, e, f: (i, 0)),
        out_shape=jax.ShapeDtypeStruct((n, d), f32),
        scratch_shapes=scratch,
        compiler_params=_cparams("parallel", "arbitrary", "arbitrary"),
        name="ffn_moe" if moe else "ffn_dense",
    )(*args)


def _lambda_init(layer):
    return 0.8 - 0.6 * math.exp(-0.3 * layer)


def _row(v, width=LANES, rows=8):
    out = jnp.zeros((rows, width), f32)
    for n, x in enumerate(v):
        x = jnp.asarray(x, f32).reshape(-1)
        out = out.at[n, :x.shape[0]].set(x)
    return out


def kernel(x_prompt, x_sample, cache_k, cache_v, cache_logf, cache_mem_k, cache_mem_v, state_wkv, state_shift,
           page_table, mem_prompt, norm_mix, w_in, b_f, diff_lam, diff_g, rw_mu, rw_w0, rw_w2, rw_a0, rw_a2, rw_g2,
           rw_kk, rw_ka, rw_rk, rw_ln_g, rw_ln_b, w_out, norm_mem, mem_norm, w_mem_q, w_mem_kv, w_mem_o, norm_ffn,
           ffn_gu, ffn_down, moe_router, moe_router_b, moe_gu, moe_down, norm_final):
    n_bp, t_p, d = x_prompt.shape
    n_bs, t_s, _ = x_sample.shape
    depth = w_in.shape[0]
    n_pages = page_table.shape[1]
    assert t_s == 4 and (n_pages * PAGE) % MOBA_BLOCK == 0 and n_pages % PAGES_PER_STEP == 0
    assert t_p % 512 == 0 and d == D_MODEL

    n_pool = cache_k.shape[1]
    k_t = jnp.transpose(cache_k, (0, 1, 3, 4, 2)).reshape(depth, n_pool, ATT_W, PAGE)
    v_t = jnp.transpose(cache_v, (0, 1, 3, 4, 2)).reshape(depth, n_pool, ATT_W, PAGE)
    lf_t = jnp.transpose(cache_logf, (0, 1, 3, 2))
    memk_t = jnp.transpose(cache_mem_k, (0, 1, 3, 4, 2)).reshape(depth, n_bs, GROUP_W, N_MEM)
    memv_t = jnp.transpose(cache_mem_v, (0, 1, 3, 4, 2)).reshape(depth, n_bs, GROUP_W, N_MEM)

    xp = x_prompt.reshape(n_bp * t_p, d)
    xs = x_sample.reshape(n_bs * t_s, d)
    zero_bias = jnp.zeros((1, LANES), f32)
    g_final = norm_final.reshape(1, d)
    outs = {k: [] for k in ("kp", "vp", "lfp", "mkp", "mvp", "wkvp", "shp", "ks", "vs", "lfs", "wkvs", "shs")}

    for l in range(depth):
        lam0 = _lambda_init(l)
        lp = diff_lam[l]
        lam = jnp.exp(jnp.sum(lp[0] * lp[1])) - jnp.exp(jnp.sum(lp[2] * lp[3])) + lam0
        g_pair = jnp.concatenate([diff_g[l], diff_g[l]])
        diff_par = _row([jnp.full((LANES,), lam), g_pair])
        w_l = w_in[l]
        w_cat = jnp.concatenate(
            [w_l[:, 3 * ATT_W + G_HEADS:], w_l[:, :3 * ATT_W],
             jnp.pad(w_l[:, 3 * ATT_W:3 * ATT_W + G_HEADS], ((0, 0), (0, LANES - G_HEADS)))], axis=1).astype(bf16)
        bias_pad = jnp.pad(b_f[l], (0, LANES - G_HEADS)).reshape(1, LANES)
        g_mix = norm_mix[l].reshape(1, d)
        w_out_b = w_out[l].astype(bf16)
        wq_b = w_mem_q[l].astype(bf16)
        wo_b = w_mem_o[l].astype(bf16)
        g_mem = norm_mem[l].reshape(1, d)
        g_ffn = norm_ffn[l].reshape(1, d)
        mu = rw_mu[l].reshape(1, RWKV_W)
        vecs = _row([rw_w0[l], rw_a0[l], rw_kk[l], rw_ka[l], rw_rk[l].reshape(-1)], width=GROUP_W)
        zpad = jnp.zeros((LANES - 64, GROUP_W), f32)
        w2p = jnp.concatenate([rw_w2[l], zpad], axis=0).astype(bf16)
        a2p = jnp.concatenate([zpad, rw_a2[l]], axis=0).astype(bf16)
        g2b = rw_g2[l].astype(bf16)
        ln = _row([rw_ln_g[l], rw_ln_b[l]], width=GROUP_W)
        last = l == depth - 1

        def rw_inputs(proj, n_b, t, shift0, tm):
            pr = proj[:, PR_COL:PR_COL + RWKV_W].reshape(n_b, t, RWKV_W)
            prev = jnp.concatenate([shift0[:, None, :], pr[:, :-1]], axis=1).reshape(n_b * t, RWKV_W)
            return pr, _rw_prep(proj, prev, mu, vecs, w2p, a2p, g2b, tm)

        def ffn(x2, tm):
            if l % 2 == 0:
                i = l // 2
                return _ffn(x2, g_ffn, ffn_gu[i:i + 1].astype(bf16), ffn_down[i:i + 1].astype(bf16), g_final,
                            tm=tm, tf=1408, final_norm=last)
            i = l // 2
            rt = jnp.pad(moe_router[i], ((0, 0), (0, LANES - N_EXPERTS)))
            rb = jnp.pad(moe_router_b[i], (0, LANES - N_EXPERTS)).reshape(1, LANES)
            return _ffn(x2, g_ffn, moe_gu[i].astype(bf16), moe_down[i].astype(bf16), g_final,
                        tm=tm, tf=1408, router=(rt, rb), final_norm=last)

        def wkv_to_bd(s):
            n = s.shape[0]
            m = jnp.zeros((n, GROUP_W, GROUP_W), f32)
            for h in range(G_HEADS):
                m = m.at[:, h * HEAD_DIM:(h + 1) * HEAD_DIM, h * HEAD_DIM:(h + 1) * HEAD_DIM].set(
                    jnp.swapaxes(s[:, h], 1, 2))
            return m

        def bd_to_wkv(m):
            return jnp.stack([jnp.swapaxes(m[:, h * HEAD_DIM:(h + 1) * HEAD_DIM, h * HEAD_DIM:(h + 1) * HEAD_DIM], 1, 2)
                              for h in range(G_HEADS)], axis=1)

        w_qkv_t = jnp.transpose(w_l[:, :3 * ATT_W]).astype(bf16)
        proj, q_t, k_t_out, v_t_out, k_b, v_tb = _norm_matmul(
            xp, g_mix, w_cat, bias_pad, tm=512, chunk=1152, logsig_tail=True, w_t=w_qkv_t, n_b=n_bp)
        pr, rw = rw_inputs(proj, n_bp, t_p, jnp.zeros((n_bp, RWKV_W), f32), 512)
        kaug = _cumsum_logf(proj, n_bp, t_p)
        g_col = jnp.broadcast_to(g_pair[:, None], (LANES, 128))
        o_a = _attn_prompt(k_b, q_t, v_tb, n_bp, t_p, "diff", 0, (diff_par, g_col), lam_scale=1.0 - lam0)
        o_b = _attn_prompt(k_b, q_t, v_tb, n_bp, t_p, "moba", 1)
        o_c = _attn_prompt(k_b, q_t, v_tb, n_bp, t_p, "fox", 2, kaug)
        o_d, m_fin = _rw_chunk(rw, jnp.zeros((n_bp, GROUP_W, GROUP_W), f32), ln, n_bp, t_p)
        kv_mem = _norm_matmul(mem_prompt.reshape(n_bp * N_MEM, d), mem_norm[l].reshape(1, d),
                              w_mem_kv[l].astype(bf16), zero_bias, tm=N_MEM, chunk=2 * GROUP_W, logsig_tail=False)
        mk_p = kv_mem[:, :GROUP_W].reshape(n_bp, N_MEM, GROUP_W)
        mv_p = kv_mem[:, GROUP_W:].reshape(n_bp, N_MEM, GROUP_W)
        to3 = lambda a: a.reshape(n_bp, t_p, a.shape[-1])
        x3 = _post_mixer(to3(xp), [to3(o_a), to3(o_b), to3(o_c), to3(o_d)], w_out_b, g_mem, wq_b,
                         jnp.swapaxes(mk_p, 1, 2), jnp.swapaxes(mv_p, 1, 2), wo_b, 512)
        xp = ffn(x3.reshape(n_bp * t_p, d), 512)
        from_t = lambda a: jnp.transpose(a.reshape(n_bp, 12, HEAD_DIM, t_p), (0, 3, 1, 2))
        outs["kp"].append(from_t(k_t_out))
        outs["vp"].append(from_t(v_t_out))
        outs["lfp"].append(proj[:, FL_COL:FL_COL + G_HEADS].reshape(n_bp, t_p, G_HEADS))
        outs["mkp"].append(mk_p.reshape(n_bp, N_MEM, G_HEADS, HEAD_DIM))
        outs["mvp"].append(mv_p.reshape(n_bp, N_MEM, G_HEADS, HEAD_DIM))
        outs["wkvp"].append(bd_to_wkv(m_fin))
        outs["shp"].append(pr[:, -1])

        proj = _norm_matmul(xs, g_mix, w_cat, bias_pad, tm=n_bs * t_s, chunk=1152, logsig_tail=True)
        pr, rw = rw_inputs(proj, n_bs, t_s, state_shift[l], n_bs * t_s)
        k_new = proj[:, K_COL:K_COL + ATT_W].reshape(n_bs, t_s, ATT_W)
        v_new = proj[:, V_COL:V_COL + ATT_W].reshape(n_bs, t_s, ATT_W)
        lf_new = proj[:, FL_COL:FL_COL + G_HEADS].reshape(n_bs, t_s, G_HEADS)
        pad_t = lambda a: jnp.pad(jnp.swapaxes(a, 1, 2), ((0, 0), (0, 0), (0, PAGE - t_s)))
        o_abc = _decode_attn(page_table, proj[:, Q_COL:Q_COL + ATT_W].reshape(n_bs, t_s, ATT_W), pad_t(k_new), pad_t(v_new),
                             pad_t(lf_new), diff_par, k_t, v_t, lf_t, l, 1.0 - lam0)
        t_pad = RW_STEP * RW_CHUNK
        rw_pad = jnp.pad(rw.reshape(n_bs, t_s, 8 * GROUP_W), ((0, 0), (0, t_pad - t_s), (0, 0)))
        o_d, m_fin = _rw_chunk(rw_pad.reshape(n_bs * t_pad, 8 * GROUP_W), wkv_to_bd(state_wkv[l]), ln, n_bs, t_pad)
        o_d = o_d.reshape(n_bs, t_pad, GROUP_W)[:, :t_s]
        x3 = _post_mixer(xs.reshape(n_bs, t_s, d), [o_abc, o_d], w_out_b, g_mem, wq_b, memk_t[l], memv_t[l], wo_b, t_s)
        xs = ffn(x3.reshape(n_bs * t_s, d), n_bs * t_s)
        outs["ks"].append(k_new.reshape(n_bs, t_s, 12, HEAD_DIM))
        outs["vs"].append(v_new.reshape(n_bs, t_s, 12, HEAD_DIM))
        outs["lfs"].append(lf_new)
        outs["wkvs"].append(bd_to_wkv(m_fin))
        outs["shs"].append(pr[:, -1])

    st = lambda k: jnp.stack(outs[k])
    return (xp.reshape(n_bp, t_p, d), xs.reshape(n_bs, t_s, d), st("kp"), st("vp"), st("lfp"), st("mkp"), st("mvp"),
            st("wkvp"), st("shp"), st("ks"), st("vs"), st("lfs"), st("wkvs"), st("shs"))
```

```python
import functools
import math

import jax
import jax.numpy as jnp
from jax import lax
from jax.experimental import pallas as pl
from jax.experimental.pallas import tpu as pltpu

f32 = jnp.float32
bf16 = jnp.bfloat16

D_MODEL = 1024
HEAD_DIM = 64
G_HEADS = 4
GROUP_W = G_HEADS * HEAD_DIM
ATT_W = 3 * GROUP_W
RWKV_W = 1024
IN_W = 3 * ATT_W + G_HEADS + RWKV_W
PROJ_W = RWKV_W + 3 * ATT_W + 128
PR_COL = 0
Q_COL = RWKV_W
K_COL = Q_COL + ATT_W
V_COL = K_COL + ATT_W
FL_COL = V_COL + ATT_W
MOBA_BLOCK = 256
MOBA_TOPK = 3
PAGE = 128
N_MEM = 256
D_FF = 2816
N_EXPERTS = 8
NORM_EPS = 1e-6
GN_EPS = 64e-5
NEG = -1e30
LANES = 128
RW_CHUNK = 64
VMEM_LIMIT = 56 * 1024 * 1024
MOE_VMEM_LIMIT = 60 * 1024 * 1024

HI = lax.Precision.HIGHEST
LOG2E = 1.4426950408889634


def _cparams(*sem):
    return pltpu.CompilerParams(dimension_semantics=sem, vmem_limit_bytes=VMEM_LIMIT)


def _dot(a, b, prec=None):
    return jnp.dot(a, b, preferred_element_type=f32, precision=prec)


def _dot3(a, b):
    a_hi = a.astype(bf16)
    b_hi = b.astype(bf16)
    a_lo = (a - a_hi.astype(f32)).astype(bf16)
    b_lo = (b - b_hi.astype(f32)).astype(bf16)
    return _dot(a_hi, b_hi) + (_dot(a_hi, b_lo) + _dot(a_lo, b_hi))


def _dot_nt(a, b, prec=None):
    return lax.dot_general(a, b, (((1,), (1,)), ((), ())), preferred_element_type=f32, precision=prec)


def _dot_tn(a, b, prec=None):
    return lax.dot_general(a, b, (((0,), (0,)), ((), ())), preferred_element_type=f32, precision=prec)


def _rms(x, g):
    return x * lax.rsqrt(jnp.mean(x * x, axis=-1, keepdims=True) + NORM_EPS) * g


def _pair_rmsnorm(o, g_pair):
    lane = lax.broadcasted_iota(jnp.int32, o.shape, 1)
    lo = lane < HEAD_DIM
    sq = o * o
    s_lo = jnp.sum(jnp.where(lo, sq, 0.0), axis=-1, keepdims=True)
    s_hi = jnp.sum(jnp.where(lo, 0.0, sq), axis=-1, keepdims=True)
    ms = jnp.where(lo, s_lo, s_hi) * (1.0 / HEAD_DIM)
    return o * lax.rsqrt(ms + NORM_EPS) * g_pair


def _norm_matmul_kernel(*refs, chunk, logsig_tail, n_t):
    x_ref, g_ref, w_ref, b_ref = refs[:4]
    wt_ref = refs[4] if n_t else None
    o_ref = refs[4 + bool(n_t)]
    t_refs = refs[5 + bool(n_t):]
    h = _rms(x_ref[...], g_ref[...]).astype(bf16)
    n_out = o_ref.shape[1]
    for c0 in range(0, n_out, chunk):
        c1 = min(c0 + chunk, n_out)
        y = _dot(h, w_ref[:, c0:c1])
        if logsig_tail and c1 == n_out:
            t0 = c1 - c0 - LANES
            tail = jax.nn.log_sigmoid(y[:, t0:] + b_ref[...])
            o_ref[:, c0:c0 + t0] = y[:, :t0]
            o_ref[:, c0 + t0:c1] = tail
        else:
            o_ref[:, c0:c1] = y
    if n_t:
        qt_ref, kt_ref, vt_ref, kb_ref, vtb_ref = t_refs
        qt_ref[...] = _dot_nt(wt_ref[0:ATT_W, :], h)
        kt_ref[...] = _dot_nt(wt_ref[ATT_W:2 * ATT_W, :], h)
        vt = _dot_nt(wt_ref[2 * ATT_W:, :], h)
        vt_ref[...] = vt
        vtb_ref[...] = vt.astype(bf16)
        kb_ref[...] = o_ref[:, K_COL:K_COL + ATT_W].astype(bf16)


def _norm_matmul(x, g, w, bias_pad, *, tm, chunk, logsig_tail, w_t=None, n_b=1):
    n, d = x.shape
    n_out = w.shape[1]
    n_t = 0 if w_t is None else w_t.shape[0] // ATT_W
    in_specs = [pl.BlockSpec((tm, d), lambda i: (i, 0)),
                pl.BlockSpec((1, d), lambda i: (0, 0)),
                pl.BlockSpec((d, n_out), lambda i: (0, 0)),
                pl.BlockSpec((1, LANES), lambda i: (0, 0))]
    args = [x, g, w, bias_pad]
    out_specs = [pl.BlockSpec((tm, n_out), lambda i: (i, 0))]
    out_shape = [jax.ShapeDtypeStruct((n, n_out), f32)]
    if n_t:
        t = n // n_b
        nt = t // tm
        in_specs.append(pl.BlockSpec(w_t.shape, lambda i: (0, 0)))
        args.append(w_t)
        t_spec = pl.BlockSpec((None, ATT_W, tm), lambda i: (i // nt, 0, i % nt))
        out_specs += [t_spec] * 3 + [pl.BlockSpec((tm, ATT_W), lambda i: (i, 0)), t_spec]
        out_shape += ([jax.ShapeDtypeStruct((n_b, ATT_W, t), f32)] * 3
                      + [jax.ShapeDtypeStruct((n, ATT_W), bf16), jax.ShapeDtypeStruct((n_b, ATT_W, t), bf16)])
    res = pl.pallas_call(
        functools.partial(_norm_matmul_kernel, chunk=chunk, logsig_tail=logsig_tail, n_t=n_t),
        grid=(n // tm,),
        in_specs=in_specs,
        out_specs=out_specs,
        out_shape=out_shape,
        compiler_params=_cparams("parallel"),
        name="norm_matmul",
    )(*args)
    return res if n_t else res[0]


def _cumsum_kernel(lf_ref, o_ref, carry_ref, *, tc):
    @pl.when(pl.program_id(1) == 0)
    def _():
        carry_ref[...] = jnp.zeros_like(carry_ref)
    r = lax.broadcasted_iota(jnp.int32, (tc, tc), 0)
    c = lax.broadcasted_iota(jnp.int32, (tc, tc), 1)
    tri = jnp.where(c <= r, 1.0, 0.0).astype(f32)
    cum = _dot(tri, lf_ref[...], HI) + carry_ref[...]
    carry_ref[...] = cum[tc - 1:tc, :]
    neg = cum * (-LOG2E)
    hi = neg.astype(bf16).astype(f32)
    mid = (neg - hi).astype(bf16).astype(f32)
    lo = (neg - hi - mid).astype(bf16).astype(f32)
    src = lax.broadcasted_iota(jnp.int32, (LANES, 2 * LANES), 0)
    dst = lax.broadcasted_iota(jnp.int32, (LANES, 2 * LANES), 1)
    out = jnp.zeros((tc, 2 * LANES), f32)
    for k, piece in enumerate((hi, mid, lo)):
        place = jnp.where((src < G_HEADS) & (dst == LANES * (src // 2) + 3 * (src % 2) + k), 1.0, 0.0)
        out = out + _dot(piece, place, HI)
    o_ref[...] = out.astype(bf16)


def _cumsum_logf(proj, n_b, t):
    tc = 256
    nt = t // tc
    return pl.pallas_call(
        functools.partial(_cumsum_kernel, tc=tc),
        grid=(n_b, nt),
        in_specs=[pl.BlockSpec((tc, LANES), lambda b, i: (b * nt + i, FL_COL // LANES))],
        out_specs=pl.BlockSpec((tc, 2 * LANES), lambda b, i: (b * nt + i, 0)),
        out_shape=jax.ShapeDtypeStruct((n_b * t, 2 * LANES), bf16),
        scratch_shapes=[pltpu.VMEM((1, LANES), f32)],
        compiler_params=_cparams("parallel", "arbitrary"),
        name="cumsum_logf",
    )(proj)


def _attn_kernel(*refs, mode, tq, tk, lam_scale):
    if mode == "diff":
        qt_ref, k_ref, vt_ref, par_ref, g_ref, o_ref = refs[:6]
    elif mode == "fox":
        qt_ref, k_ref, vt_ref, kaug_ref, o_ref = refs[:5]
    else:
        qt_ref, k_ref, vt_ref, o_ref = refs[:4]
    n_in = {"diff": 6, "fox": 5, "moba": 4}[mode]
    m_s, l_s, acc_s, sc_s = (refs[n_in:n_in + 2], refs[n_in + 2:n_in + 4], refs[n_in + 4:n_in + 8],
                             refs[n_in + 8:n_in + 10])
    if mode == "moba":
        kmean_s, sel_s = refs[n_in + 10], refs[n_in + 11:n_in + 13]
    i = pl.program_id(1)
    nvar = 4 if mode == "diff" else 2
    cols = nvar * tq
    drow = lax.broadcasted_iota(jnp.int32, (LANES, tq), 0)
    qpos = i * tq + (lax.broadcasted_iota(jnp.int32, (1, cols), 1) & (tq - 1))
    qs = []
    qs32 = []
    for hp in range(2):
        qt = qt_ref[hp * LANES:(hp + 1) * LANES, :]
        if mode == "diff":
            half = HEAD_DIM // 2
            scale = half ** -0.5 * LOG2E
            parts = [jnp.where((drow >= v * half) & (drow < (v + 1) * half), qt, 0.0) for v in range(4)]
        else:
            scale = HEAD_DIM ** -0.5 * LOG2E
            parts = [jnp.where(drow < HEAD_DIM, qt, 0.0), jnp.where(drow >= HEAD_DIM, qt, 0.0)]
        q32 = jnp.concatenate(parts, axis=1) * scale
        qs32.append(q32)
        if mode == "fox":
            r = lax.broadcasted_iota(jnp.int32, (LANES, cols), 0)
            c = lax.broadcasted_iota(jnp.int32, (LANES, cols), 1)
            ind = jnp.where(((r < 3) & (c < tq)) | ((r >= 3) & (r < 6) & (c >= tq)), 1.0, 0.0)
            qs.append(jnp.concatenate([q32, ind], axis=0).astype(bf16))
        else:
            qs.append(q32.astype(bf16))
        m_s[hp][...] = jnp.full(m_s[hp].shape, NEG, f32)
        l_s[hp][...] = jnp.zeros(l_s[hp].shape, f32)
        for hh in range(2):
            acc_s[2 * hp + hh][...] = jnp.zeros(acc_s[2 * hp + hh].shape, f32)

    if mode == "moba":
        nb = k_ref.shape[0] // MOBA_BLOCK

        @pl.when(i == 0)
        def _():
            kmean_s[...] = jnp.zeros(kmean_s.shape, f32)

            def blk(n, _):
                kb = k_ref[pl.ds(pl.multiple_of(n * MOBA_BLOCK, MOBA_BLOCK), MOBA_BLOCK), :]
                kmean_s[pl.ds(n, 1), :] = jnp.mean(kb.astype(f32), axis=0, keepdims=True)
                return 0
            lax.fori_loop(0, nb, blk, 0)

        nidx = lax.broadcasted_iota(jnp.int32, (LANES, cols), 0)
        nidx_f = nidx.astype(f32)
        own = qpos // MOBA_BLOCK
        own_f = own.astype(f32)
        for hp in range(2):
            gate = _dot(kmean_s[:, hp * LANES:(hp + 1) * LANES], qs32[hp], HI)
            g = jnp.where(nidx < own, gate, NEG)
            sel = jnp.zeros((LANES, cols), f32)
            for _ in range(MOBA_TOPK):
                mx = jnp.max(g, axis=0, keepdims=True)
                first = jnp.min(jnp.where(g == mx, nidx_f, 1e9), axis=0, keepdims=True)
                pick = nidx_f == first
                sel = jnp.where(pick & (first < own_f), 1.0, sel)
                g = jnp.where(pick, -3e38, g)
            sel_s[hp][...] = sel

    def scores(j, hp):
        start = pl.multiple_of(j * tk, tk)
        lanes = slice(hp * LANES, (hp + 1) * LANES)
        kb = k_ref[pl.ds(start, tk), lanes]
        if mode == "fox":
            kb = jnp.concatenate([kb, kaug_ref[pl.ds(start, tk), lanes]], axis=1)
        return _dot(kb, qs[hp])

    def absorb(s, j, hp, diagonal):
        start = pl.multiple_of(j * tk, tk)
        valid = None
        if mode == "moba" and not diagonal:
            valid = sel_s[hp][pl.ds(j, 1), :] > 0.5
        if diagonal:
            kpos = start + lax.broadcasted_iota(jnp.int32, (tk, 1), 0)
            valid = kpos <= qpos
        if valid is not None:
            s = jnp.where(valid, s, NEG)
        m_old = m_s[hp][...]
        m_new = jnp.maximum(m_old, jnp.max(s, axis=0, keepdims=True))
        alpha = jnp.exp2(m_old - m_new)
        p = jnp.exp2(s - m_new)
        if valid is not None:
            p = jnp.where(valid, p, 0.0)
        l_s[hp][...] = alpha * l_s[hp][...] + jnp.sum(p, axis=0, keepdims=True)
        pb = p.astype(bf16)
        hc = cols // 2
        for hh in range(2):
            vt = vt_ref[hp * LANES + hh * HEAD_DIM:hp * LANES + (hh + 1) * HEAD_DIM, pl.ds(start, tk)]
            acc = acc_s[2 * hp + hh]
            acc[...] = alpha[:, hh * hc:(hh + 1) * hc] * acc[...] + _dot(vt, pb[:, hh * hc:(hh + 1) * hc])
        m_s[hp][...] = m_new

    n_full = (i * tq) // tk
    for hp in range(2):
        sc_s[hp][...] = scores(0, hp)

    def body(j, _):
        for hp in range(2):
            nxt = scores(j + 1, hp)
            absorb(sc_s[hp][...], j, hp, False)
            sc_s[hp][...] = nxt
        return 0
    lax.fori_loop(0, n_full, body, 0)
    for hp in range(2):
        absorb(sc_s[hp][...], n_full, hp, True)

    hc = cols // 2
    for hp in range(2):
        inv_l = 1.0 / l_s[hp][...]
        a = [acc_s[2 * hp + hh][...] * inv_l[:, hh * hc:(hh + 1) * hc] for hh in range(2)]
        if mode == "diff":
            lam = par_ref[0:1, 0:1]
            o = jnp.concatenate([a_h[:, 0:tq] - lam * a_h[:, tq:2 * tq] for a_h in a], axis=0)
            lo = drow < HEAD_DIM
            sq = o * o
            s_lo = jnp.sum(jnp.where(lo, sq, 0.0), axis=0, keepdims=True)
            s_hi = jnp.sum(jnp.where(lo, 0.0, sq), axis=0, keepdims=True)
            ms = jnp.where(lo, s_lo, s_hi) * (1.0 / HEAD_DIM)
            o = o * lax.rsqrt(ms + NORM_EPS) * g_ref[...] * lam_scale
        else:
            o = jnp.concatenate(a, axis=0)
        o_ref[:, hp * LANES:(hp + 1) * LANES] = o.T


def _attn_prompt(k_b, q_t, v_t, n_b, t, mode, group, extra=None, lam_scale=1.0):
    tk = MOBA_BLOCK if mode == "moba" else 512
    tq = 128 if mode == "diff" else 256
    nq = t // tq
    nvar = 4 if mode == "diff" else 2
    cols = nvar * tq
    in_specs = [pl.BlockSpec((None, GROUP_W, tq), lambda b, i: (b, group, i)),
                pl.BlockSpec((t, GROUP_W), lambda b, i: (b, group)),
                pl.BlockSpec((None, GROUP_W, t), lambda b, i: (b, group, 0))]
    args = [q_t, k_b, v_t]
    scratch = ([pltpu.VMEM((1, cols), f32)] * 4 + [pltpu.VMEM((HEAD_DIM, cols // 2), f32)] * 4
               + [pltpu.VMEM((tk, cols), f32)] * 2)
    if mode == "diff":
        in_specs += [pl.BlockSpec((8, LANES), lambda b, i: (0, 0)),
                     pl.BlockSpec((LANES, tq), lambda b, i: (0, 0))]
        args += list(extra)
    elif mode == "fox":
        in_specs.append(pl.BlockSpec((t, GROUP_W), lambda b, i: (b, 0)))
        args.append(extra)
    else:
        scratch += [pltpu.VMEM((LANES, GROUP_W), f32)] + [pltpu.VMEM((LANES, cols), f32)] * 2
    return pl.pallas_call(
        functools.partial(_attn_kernel, mode=mode, tq=tq, tk=tk, lam_scale=lam_scale),
        grid=(n_b, nq),
        in_specs=in_specs,
        out_specs=pl.BlockSpec((tq, GROUP_W), lambda b, i: (b * nq + i, 0)),
        out_shape=jax.ShapeDtypeStruct((n_b * t, GROUP_W), f32),
        scratch_shapes=scratch,
        compiler_params=_cparams("parallel", "arbitrary"),
        name="attn_" + mode,
    )(*args)


def _head_ones(n):
    r = lax.broadcasted_iota(jnp.int32, (n, n), 0) // HEAD_DIM
    c = lax.broadcasted_iota(jnp.int32, (n, n), 1) // HEAD_DIM
    return jnp.where(r == c, 1.0, 0.0).astype(f32)


def _rw_prep_kernel(pr_ref, prev_ref, mu_ref, vec_ref, w2_ref, a2_ref, g2_ref, o_ref):
    pr = pr_ref[...]
    xm = pr + (prev_ref[...] - pr) * mu_ref[...]
    r = xm[:, 0:GROUP_W]
    k = xm[:, GROUP_W:2 * GROUP_W]
    v = xm[:, 2 * GROUP_W:3 * GROUP_W]
    x_lora = xm[:, 3 * GROUP_W:3 * GROUP_W + LANES]
    x_gate = xm[:, 3 * GROUP_W + LANES:]
    w0, a0, kk_p, ka, rk = (vec_ref[n:n + 1, :] for n in range(5))
    w_lin = w0 + _dot(jnp.tanh(x_lora).astype(bf16), w2_ref[...])
    w_log = -jax.nn.softplus(-w_lin) - 0.5
    lw = -jnp.exp(w_log)
    a = jax.nn.sigmoid(a0 + _dot(x_lora.astype(bf16), a2_ref[...]))
    g = _dot(jax.nn.sigmoid(x_gate).astype(bf16), g2_ref[...])
    ones = _head_ones(GROUP_W)
    kk = k * kk_p
    kk = kk * lax.rsqrt(_dot(kk * kk, ones, HI) + 1e-12)
    kh = k * (1.0 + (a - 1.0) * ka)
    bonus = _dot(r * kh * rk, ones, HI) * v
    for n, val in enumerate((r, lw, kh, v, kk, kk * a, g, bonus)):
        o_ref[:, n * GROUP_W:(n + 1) * GROUP_W] = val


def _rw_prep(proj, prev, mu, vecs, w2p, a2p, g2, tm):
    n = proj.shape[0]
    full = lambda shape: pl.BlockSpec(shape, lambda i: (0,) * len(shape))
    return pl.pallas_call(
        _rw_prep_kernel,
        grid=(n // tm,),
        in_specs=[pl.BlockSpec((tm, RWKV_W), lambda i: (i, PR_COL // RWKV_W)),
                  pl.BlockSpec((tm, RWKV_W), lambda i: (i, 0)),
                  full((1, RWKV_W)), full((8, GROUP_W)), full((LANES, GROUP_W)), full((LANES, GROUP_W)),
                  full((LANES, GROUP_W))],
        out_specs=pl.BlockSpec((tm, 8 * GROUP_W), lambda i: (i, 0)),
        out_shape=jax.ShapeDtypeStruct((n, 8 * GROUP_W), f32),
        compiler_params=_cparams("parallel"),
        name="rw_prep",
    )(proj, prev, mu, vecs, w2p, a2p, g2)


RW_STEP = 2


def _rw_chunk_kernel(x_ref, m0_ref, ln_ref, o_ref, mfin_ref, m_s, *, n_sub):
    c = pl.program_id(1)
    nc = pl.num_programs(1)
    C = RW_CHUNK
    W = GROUP_W

    @pl.when(c == 0)
    def _():
        m_s[...] = m0_ref[...]

    tr = lax.broadcasted_iota(jnp.int32, (C, C), 0)
    tc = lax.broadcasted_iota(jnp.int32, (C, C), 1)
    tri = jnp.where(tc <= tr, 1.0, 0.0).astype(f32)
    ri = lax.broadcasted_iota(jnp.int32, (W, W), 0)
    ci = lax.broadcasted_iota(jnp.int32, (W, W), 1)
    same = (ri // C) == (ci // HEAD_DIM)
    same_t = (ri // C) == (ci // C)
    strict = same_t & ((ci % C) < (ri % C))
    incl = same_t & ((ci % C) <= (ri % C))
    eye = jnp.where(ri == ci, 1.0, 0.0).astype(f32)
    mean_mat = _head_ones(W) * (1.0 / HEAD_DIM)

    def sm(x):
        return jnp.where(same, jnp.concatenate([x] * G_HEADS, axis=0), 0.0)

    def mm(x, y):
        return _dot(x.astype(bf16), y.astype(bf16))

    def chunk_ops(u):
        r, lw, kh, v, kk, b = (x_ref[u * C:(u + 1) * C, n * W:(n + 1) * W] for n in range(6))
        cum = _dot(tri, lw, HI)
        inv = jnp.exp(-cum)
        g_end = jnp.exp(cum[C - 1:C, :])
        a_sm = sm(-kk * jnp.exp(cum - lw))
        r_sm = sm(r * jnp.exp(cum))
        b_sm = sm(b * inv)
        k_sm = sm(kh * inv)
        v_sm = sm(v)
        gram = _dot_nt(jnp.concatenate([a_sm, r_sm], axis=0).astype(bf16),
                       jnp.concatenate([b_sm, k_sm], axis=0).astype(bf16))
        n_ab = jnp.where(strict, gram[:W, :W], 0.0)
        a_ak = jnp.where(strict, gram[:W, W:], 0.0)
        a_rb = jnp.where(incl, gram[W:, :W], 0.0)
        a_rk = jnp.where(incl, gram[W:, W:], 0.0)
        t_inv = eye + n_ab
        n_pow = n_ab
        for _ in range(int(math.log2(C)) - 1):
            n_pow = mm(n_pow, n_pow)
            t_inv = t_inv + mm(n_pow, t_inv)
        akv = mm(a_ak, v_sm)
        tx = mm(t_inv, jnp.concatenate([a_sm, akv], axis=1))
        y = mm(a_rb, tx)
        w_o = r_sm + y[:, :W]
        o_0 = y[:, W:] + mm(a_rk, v_sm)
        bt = (b_sm * g_end).astype(bf16)
        kt = (k_sm * g_end).astype(bf16)
        pq = _dot_tn(bt, tx.astype(bf16))
        p_mat = jnp.where(ri == ci, g_end, 0.0) + pq[:, :W]
        q_mat = pq[:, W:] + _dot_tn(kt, v_sm.astype(bf16))
        return w_o, o_0, p_mat, q_mat

    ops = [chunk_ops(u) for u in range(n_sub)]
    m = m_s[...]
    for u, (w_o, o_0, p_mat, q_mat) in enumerate(ops):
        o_sm = _dot3(w_o, m) + o_0
        m = _dot3(p_mat, m) + q_mat
        o = o_sm[0:C] + o_sm[C:2 * C] + o_sm[2 * C:3 * C] + o_sm[3 * C:4 * C]
        mu = _dot(o, mean_mat, HI)
        d = o - mu
        var = _dot(d * d, mean_mat, HI)
        o_n = d * lax.rsqrt(var + GN_EPS) * ln_ref[0:1, :] + ln_ref[1:2, :]
        g = x_ref[u * C:(u + 1) * C, 6 * W:7 * W]
        bonus = x_ref[u * C:(u + 1) * C, 7 * W:8 * W]
        o_ref[u * C:(u + 1) * C, :] = (o_n + bonus) * g
    m_s[...] = m

    @pl.when(c == nc - 1)
    def _():
        mfin_ref[...] = m


def _rw_chunk(rw, m0, ln, n_b, t):
    n_sub = RW_STEP if t % (RW_STEP * RW_CHUNK) == 0 else 1
    rows = n_sub * RW_CHUNK
    nc = t // rows
    return pl.pallas_call(
        functools.partial(_rw_chunk_kernel, n_sub=n_sub),
        grid=(n_b, nc),
        in_specs=[pl.BlockSpec((rows, 8 * GROUP_W), lambda b, c: (b * nc + c, 0)),
                  pl.BlockSpec((None, GROUP_W, GROUP_W), lambda b, c: (b, 0, 0)),
                  pl.BlockSpec((8, GROUP_W), lambda b, c: (0, 0))],
        out_specs=[pl.BlockSpec((rows, GROUP_W), lambda b, c: (b * nc + c, 0)),
                   pl.BlockSpec((None, GROUP_W, GROUP_W), lambda b, c: (b, 0, 0))],
        out_shape=[jax.ShapeDtypeStruct((n_b * t, GROUP_W), f32),
                   jax.ShapeDtypeStruct((n_b, GROUP_W, GROUP_W), f32)],
        scratch_shapes=[pltpu.VMEM((GROUP_W, GROUP_W), f32)],
        compiler_params=_cparams("parallel", "arbitrary"),
        name="rw_chunk",
    )(rw, m0, ln)


N_COLS = 64
COL_B, COL_C = 32, 48
PAGES_PER_STEP = 8


def _col_maps(shape, axis):
    c = lax.broadcasted_iota(jnp.int32, shape, axis)
    tok = c % 4
    head = jnp.where(c < COL_B, c // 8, jnp.where(c < COL_C, 4 + (c - COL_B) // 4, 8 + (c - COL_C) // 4))
    half = (c // 4) % 2
    lo = head * HEAD_DIM + jnp.where(c < COL_B, half * (HEAD_DIM // 2), 0)
    hi = jnp.where(c < COL_B, lo + HEAD_DIM // 2, lo + HEAD_DIM)
    return c, tok, head, lo, hi


def _decode_kernel(*refs, lam_scale, nj):
    npg = PAGES_PER_STEP
    pt_ref, q_ref, kn_ref, vn_ref, ln_ref, par_ref = refs[:6]
    k_refs, v_refs, l_refs = refs[6:6 + npg], refs[6 + npg:6 + 2 * npg], refs[6 + 2 * npg:6 + 3 * npg]
    o_ref, qbd_s, m_s, l_s, acc_s, carry_s, bm_s, bl_s, bg_s, bacc_s = refs[6 + 3 * npg:]
    j = pl.program_id(1)
    nblk = npg // 2
    c1, tok1, head1, lo1, hi1 = _col_maps((N_COLS, 1), 0)

    @pl.when(j == 0)
    def _():
        q4 = q_ref[...]
        lanes = lax.broadcasted_iota(jnp.int32, (N_COLS, ATT_W), 1)
        qb = jnp.zeros((N_COLS, ATT_W), f32)
        for t in range(4):
            qb = jnp.where(tok1 == t, q4[t:t + 1, :], qb)
        scale = jnp.where(c1 < COL_B, (HEAD_DIM // 2) ** -0.5, HEAD_DIM ** -0.5)
        qbd_s[...] = (jnp.where((lanes >= lo1) & (lanes < hi1), qb, 0.0) * scale).astype(bf16)
        m_s[...] = jnp.full(m_s.shape, NEG, f32)
        l_s[...] = jnp.zeros(l_s.shape, f32)
        acc_s[...] = jnp.zeros(acc_s.shape, f32)
        carry_s[...] = jnp.zeros(carry_s.shape, f32)
        bm_s[...] = jnp.zeros(bm_s.shape, f32)
        bl_s[...] = jnp.zeros(bl_s.shape, f32)
        bg_s[...] = jnp.zeros(bg_s.shape, f32)

    qbd = qbd_s[...]
    is_c = c1 >= COL_C

    def forget_bias(cum):
        bias = jnp.zeros((N_COLS, cum.shape[1]), f32)
        for h in range(G_HEADS):
            bias = jnp.where(is_c & (head1 == 8 + h), -cum[h:h + 1, :], bias)
        return bias

    def upper(n):
        r = lax.broadcasted_iota(jnp.int32, (n, n), 0)
        c = lax.broadcasted_iota(jnp.int32, (n, n), 1)
        return jnp.where(r <= c, 1.0, 0.0).astype(f32)

    def merge(parts):
        m_old = m_s[...]
        m_new = m_old
        for m_k, _, _ in parts:
            m_new = jnp.maximum(m_new, m_k)
        a = jnp.exp(m_old - m_new)
        l_new = a * l_s[...]
        acc_new = a * acc_s[...]
        for m_k, l_k, acc_k in parts:
            e_k = jnp.exp(m_k - m_new)
            l_new = l_new + e_k * l_k
            acc_new = acc_new + e_k * acc_k
        l_s[...] = l_new
        acc_s[...] = acc_new
        m_s[...] = m_new

    kw = 2 * PAGE
    lf = jnp.concatenate([jnp.concatenate([l_refs[2 * k][...], l_refs[2 * k + 1][...]], axis=1) for k in range(nblk)],
                         axis=0)
    cum_loc = _dot(lf, upper(kw), HI)
    carry = carry_s[0:G_HEADS, 0:1]
    blane = lax.broadcasted_iota(jnp.int32, (COL_C - COL_B, LANES), 1)
    parts = []
    for k in range(nblk):
        cum = cum_loc[G_HEADS * k:G_HEADS * (k + 1), :] + carry
        carry = cum[:, kw - 1:kw]
        s = jnp.concatenate([_dot(qbd, k_refs[2 * k][...].astype(bf16)),
                             _dot(qbd, k_refs[2 * k + 1][...].astype(bf16))], axis=1)
        gate = jnp.sum(s[COL_B:COL_C], axis=-1, keepdims=True) * (1.0 / MOBA_BLOCK)
        s = s + forget_bias(cum)
        m_blk = jnp.max(s, axis=-1, keepdims=True)
        p = jnp.exp(s - m_blk)
        l_blk = jnp.sum(p, axis=-1, keepdims=True)
        pb = p.astype(bf16)
        acc_blk = (_dot_nt(pb[:, :PAGE], v_refs[2 * k][...].astype(bf16))
                   + _dot_nt(pb[:, PAGE:], v_refs[2 * k + 1][...].astype(bf16)))
        parts.append((m_blk, l_blk, acc_blk))
        n = j * nblk + k
        bm_s[...] = jnp.where(blane == n, m_blk[COL_B:COL_C], bm_s[...])
        bl_s[...] = jnp.where(blane == n, l_blk[COL_B:COL_C], bl_s[...])
        bg_s[...] = jnp.where(blane == n, gate, bg_s[...])
        bacc_s[n] = acc_blk[COL_B:COL_C]
    carry_s[0:G_HEADS, :] = jnp.broadcast_to(carry, (G_HEADS, LANES))
    merge(parts)

    @pl.when(j == nj - 1)
    def _():
        sn = _dot(qbd, kn_ref[...].astype(bf16))
        cum_n = _dot(ln_ref[...], upper(PAGE), HI) + carry_s[0:G_HEADS, 0:1]
        klane = lax.broadcasted_iota(jnp.int32, (N_COLS, PAGE), 1)
        sn = jnp.where(klane <= tok1, sn + forget_bias(cum_n), NEG)
        m_n = jnp.max(sn, axis=-1, keepdims=True)
        p_n = jnp.exp(sn - m_n)
        l_n = jnp.sum(p_n, axis=-1, keepdims=True)
        acc_n = _dot_nt(p_n.astype(bf16), vn_ref[...].astype(bf16))
        merge([(m_n, l_n, acc_n)])

        nb = nj * nblk
        bgate = jnp.where(blane < nb, bg_s[...], NEG)
        blane_f = blane.astype(f32)
        sel = jnp.zeros(bgate.shape, f32)
        for _ in range(MOBA_TOPK):
            mx = jnp.max(bgate, axis=-1, keepdims=True)
            first = jnp.min(jnp.where(bgate == mx, blane_f, 1e9), axis=-1, keepdims=True)
            pick = blane_f == first
            sel = jnp.where(pick & (first < nb), 1.0, sel)
            bgate = jnp.where(pick, -3e38, bgate)
        on = sel > 0.5
        mb_n, lb_n, accb_n = m_n[COL_B:COL_C], l_n[COL_B:COL_C], acc_n[COL_B:COL_C]
        m_tot = jnp.maximum(jnp.max(jnp.where(on, bm_s[...], NEG), axis=-1, keepdims=True), mb_n)
        wgt = jnp.where(on, jnp.exp(bm_s[...] - m_tot), 0.0)
        e_n = jnp.exp(mb_n - m_tot)
        l_b = jnp.sum(wgt * bl_s[...], axis=-1, keepdims=True) + e_n * lb_n

        def add_block(n, acc):
            w_n = jnp.sum(jnp.where(blane == n, wgt, 0.0), axis=-1, keepdims=True)
            return acc + w_n * bacc_s[n]
        acc_b = lax.fori_loop(0, nb, add_block, e_n * accb_n)
        l_s[COL_B:COL_C, :] = l_b
        acc_s[COL_B:COL_C, :] = acc_b

        lam = par_ref[0:1, 0:1]
        coef = jnp.where((c1 < COL_B) & ((c1 // 4) % 2 == 1), -lam, 1.0) / l_s[...]
        lanes = lax.broadcasted_iota(jnp.int32, (N_COLS, ATT_W), 1)
        keep = (lanes >= head1 * HEAD_DIM) & (lanes < (head1 + 1) * HEAD_DIM)
        contrib = jnp.where(keep, acc_s[...] * coef, 0.0)
        _, tok_r, _, _, _ = _col_maps((8, N_COLS), 1)
        trow = lax.broadcasted_iota(jnp.int32, (8, N_COLS), 0)
        o_tok = _dot(jnp.where(tok_r == trow, 1.0, 0.0).astype(f32), contrib, HI)[0:4]
        o_ref[:, 2 * LANES:] = o_tok[:, 2 * LANES:]
        for hp in range(2):
            o_ref[:, hp * LANES:(hp + 1) * LANES] = (
                _pair_rmsnorm(o_tok[:, hp * LANES:(hp + 1) * LANES], par_ref[1:2, :]) * lam_scale)


def _decode_attn(page_table, q_s, kn_t, vn_t, ln_t, par, k_t, v_t, lf_t, layer, lam_scale):
    n_b, n_pages = page_table.shape
    npg = PAGES_PER_STEP
    nj = n_pages // npg
    nblocks = n_pages // 2

    def page_spec(rows, g):
        return pl.BlockSpec((None, None, rows, PAGE), lambda b, j, pt: (layer, pt[b, npg * j + g], 0, 0))
    new_spec = lambda rows: pl.BlockSpec((None, rows, PAGE), lambda b, j, pt: (b, 0, 0))
    nbc = COL_C - COL_B
    return pl.pallas_call(
        functools.partial(_decode_kernel, lam_scale=lam_scale, nj=nj),
        grid_spec=pltpu.PrefetchScalarGridSpec(
            num_scalar_prefetch=1,
            grid=(n_b, nj),
            in_specs=([pl.BlockSpec((None, 4, ATT_W), lambda b, j, pt: (b, 0, 0)),
                       new_spec(ATT_W), new_spec(ATT_W), new_spec(G_HEADS),
                       pl.BlockSpec((8, LANES), lambda b, j, pt: (0, 0))]
                      + [page_spec(ATT_W, g) for g in range(npg)] * 2
                      + [page_spec(G_HEADS, g) for g in range(npg)]),
            out_specs=pl.BlockSpec((None, 4, ATT_W), lambda b, j, pt: (b, 0, 0)),
            scratch_shapes=[pltpu.VMEM((N_COLS, ATT_W), bf16),
                            pltpu.VMEM((N_COLS, 1), f32), pltpu.VMEM((N_COLS, 1), f32),
                            pltpu.VMEM((N_COLS, ATT_W), f32), pltpu.VMEM((8, LANES), f32),
                            pltpu.VMEM((nbc, LANES), f32), pltpu.VMEM((nbc, LANES), f32),
                            pltpu.VMEM((nbc, LANES), f32), pltpu.VMEM((nblocks, nbc, ATT_W), f32)]),
        out_shape=jax.ShapeDtypeStruct((n_b, 4, ATT_W), f32),
        compiler_params=_cparams("parallel", "arbitrary"),
        name="decode_attn",
    )(page_table, q_s, kn_t, vn_t, ln_t, par, *([k_t] * npg), *([v_t] * npg), *([lf_t] * npg))


def _post_mixer_kernel(*refs, widths):
    n_o = len(widths)
    x_ref = refs[0]
    o_refs = refs[1:1 + n_o]
    w_out_ref, g_ref, wq_ref, mk_ref, mv_ref, wo_ref, y_ref = refs[1 + n_o:]
    x = x_ref[...]
    r0 = 0
    for o_r, w in zip(o_refs, widths):
        x = x + _dot(o_r[...].astype(bf16), w_out_ref[r0:r0 + w, :])
        r0 += w
    h = _rms(x, g_ref[...]).astype(bf16)
    q = _dot(h, wq_ref[...]) * (HEAD_DIM ** -0.5)
    lane = lax.broadcasted_iota(jnp.int32, q.shape, 1)
    mk = mk_ref[...].astype(bf16)
    mv = mv_ref[...].astype(bf16)
    o2 = jnp.zeros(q.shape, f32)
    for hd in range(G_HEADS):
        in_head = (lane >= hd * HEAD_DIM) & (lane < (hd + 1) * HEAD_DIM)
        s = _dot(jnp.where(in_head, q, 0.0).astype(bf16), mk)
        p = jnp.exp(s - jnp.max(s, axis=-1, keepdims=True))
        p = p / jnp.sum(p, axis=-1, keepdims=True)
        o2 = jnp.where(in_head, _dot_nt(p.astype(bf16), mv), o2)
    y_ref[...] = x + _dot(o2.astype(bf16), wo_ref[...])


def _post_mixer(x3, o_list, w_out, g, wq, mk_t, mv_t, wo, tm):
    n_b, t, d = x3.shape
    widths = tuple(o.shape[-1] for o in o_list)
    full = lambda a: pl.BlockSpec(a.shape, lambda b, i: (0,) * a.ndim)
    tok = lambda w: pl.BlockSpec((None, tm, w), lambda b, i: (b, i, 0))
    mem = pl.BlockSpec((None, GROUP_W, N_MEM), lambda b, i: (b, 0, 0))
    return pl.pallas_call(
        functools.partial(_post_mixer_kernel, widths=widths),
        grid=(n_b, t // tm),
        in_specs=[tok(d)] + [tok(w) for w in widths] + [full(w_out), full(g), full(wq), mem, mem, full(wo)],
        out_specs=tok(d),
        out_shape=jax.ShapeDtypeStruct(x3.shape, f32),
        compiler_params=_cparams("parallel", "parallel"),
        name="post_mixer",
    )(x3, *o_list, w_out, g, wq, mk_t, mv_t, wo)


def _ffn_kernel(*refs, moe, final_norm):
    if moe:
        x_ref, g_ref, rt_ref, rb_ref, wg_ref, wu_ref, wd_ref, gf_ref, y_ref, h_s, acc_s, tmp_s, gate_s = refs
    else:
        x_ref, g_ref, wg_ref, wu_ref, wd_ref, gf_ref, y_ref, h_s, acc_s = refs
    e = pl.program_id(1)
    f = pl.program_id(2)
    ne = pl.num_programs(1)
    nf = pl.num_programs(2)

    @pl.when((e == 0) & (f == 0))
    def _():
        h = _rms(x_ref[...], g_ref[...])
        h_s[...] = h.astype(bf16)
        acc_s[...] = jnp.zeros(acc_s.shape, f32)
        if moe:
            logits = _dot(h, rt_ref[...], HI) + rb_ref[...]
            lane = lax.broadcasted_iota(jnp.int32, logits.shape, 1)
            lane_f = lane.astype(f32)
            lg = jnp.where(lane < N_EXPERTS, logits, -3e38)
            v1 = jnp.max(lg, axis=-1, keepdims=True)
            i1 = jnp.min(jnp.where(lg == v1, lane_f, 1e9), axis=-1, keepdims=True)
            lg2 = jnp.where(lane_f == i1, -3e38, lg)
            v2 = jnp.max(lg2, axis=-1, keepdims=True)
            i2 = jnp.min(jnp.where(lg2 == v2, lane_f, 1e9), axis=-1, keepdims=True)
            e2 = jnp.exp(v2 - v1)
            den = 1.0 + e2
            gate_s[...] = jnp.where(lane_f == i1, 1.0 / den, jnp.where(lane_f == i2, e2 / den, 0.0))

    h = h_s[...]
    gg = _dot(h, wg_ref[...])
    uu = _dot(h, wu_ref[...])
    act = (gg * jax.nn.sigmoid(gg) * uu).astype(bf16)
    part = _dot(act, wd_ref[...])
    if moe:
        @pl.when(f == 0)
        def _():
            tmp_s[...] = part

        @pl.when(f > 0)
        def _():
            tmp_s[...] = tmp_s[...] + part

        @pl.when(f == nf - 1)
        def _():
            lane = lax.broadcasted_iota(jnp.int32, gate_s.shape, 1)
            ge = jnp.sum(jnp.where(lane == e, gate_s[...], 0.0), axis=-1, keepdims=True)
            acc_s[...] = acc_s[...] + ge * tmp_s[...]
    else:
        acc_s[...] = acc_s[...] + part

    @pl.when((e == ne - 1) & (f == nf - 1))
    def _():
        y = x_ref[...] + acc_s[...]
        if final_norm:
            y = _rms(y, gf_ref[...])
        y_ref[...] = y


def _ffn(x, g, w_gu, w_d, g_final, *, tm, tf, router=None, final_norm=False):
    n, d = x.shape
    moe = router is not None
    ne = w_gu.shape[0]
    nf = D_FF // tf
    full = lambda a: pl.BlockSpec(a.shape, lambda i, e, f: (0,) * a.ndim)
    in_specs = [pl.BlockSpec((tm, d), lambda i, e, f: (i, 0)), full(g)]
    args = [x, g]
    if moe:
        in_specs += [full(router[0]), full(router[1])]
        args += list(router)
    in_specs += [pl.BlockSpec((None, d, tf), lambda i, e, f: (e, 0, f)),
                 pl.BlockSpec((None, d, tf), lambda i, e, f: (e, 0, nf + f)),
                 pl.BlockSpec((None, tf, d), lambda i, e, f: (e, f, 0)),
                 full(g_final)]
    args += [w_gu, w_gu, w_d, g_final]
    scratch = [pltpu.VMEM((tm, d), bf16), pltpu.VMEM((tm, d), f32)]
    if moe:
        scratch += [pltpu.VMEM((tm, d), f32), pltpu.VMEM((tm, LANES), f32)]
    return pl.pallas_call(
        functools.partial(_ffn_kernel, moe=moe, final_norm=final_norm),
        grid=(n // tm, ne, nf),
        in_specs=in_specs,
        out_specs=pl.BlockSpec((tm, d), lambda i, e, f: (i, 0)),
        out_shape=jax.ShapeDtypeStruct((n, d), f32),
        scratch_shapes=scratch,
        compiler_params=_cparams("parallel", "arbitrary", "arbitrary"),
        name="ffn_moe" if moe else "ffn_dense",
    )(*args)


MOE_TM = 1024
MOE_CAP = 384


def _moe_kernel(x_ref, g_ref, rt_ref, rb_ref, wg_ref, wu_ref, wd_ref, gf_ref, y_ref,
                h_s, gate_s, memb_s, rank_s, ct_s, hc_s, tmp_s, *, final_norm):
    e = pl.program_id(1)
    f = pl.program_id(2)
    ne = pl.num_programs(1)
    nf = pl.num_programs(2)
    tm = x_ref.shape[0]
    n_pass = -(-tm // MOE_CAP)

    @pl.when((e == 0) & (f == 0))
    def _():
        h = _rms(x_ref[...], g_ref[...])
        h_s[...] = h.astype(bf16)
        logits = _dot(h, rt_ref[...], HI) + rb_ref[...]
        lane_f = lax.broadcasted_iota(jnp.int32, logits.shape, 1).astype(f32)
        lg = jnp.where(lane_f < N_EXPERTS, logits, -3e38)
        v1 = jnp.max(lg, axis=-1, keepdims=True)
        i1 = jnp.min(jnp.where(lg == v1, lane_f, 1e9), axis=-1, keepdims=True)
        lg2 = jnp.where(lane_f == i1, -3e38, lg)
        v2 = jnp.max(lg2, axis=-1, keepdims=True)
        i2 = jnp.min(jnp.where(lg2 == v2, lane_f, 1e9), axis=-1, keepdims=True)
        e2 = jnp.exp(v2 - v1)
        den = 1.0 + e2
        gate_s[...] = jnp.where(lane_f == i1, 1.0 / den, jnp.where(lane_f == i2, e2 / den, 0.0))
        memb = jnp.where((lane_f == i1) | (lane_f == i2), 1.0, 0.0)
        memb_s[...] = memb
        r = lax.broadcasted_iota(jnp.int32, (tm, tm), 0)
        c = lax.broadcasted_iota(jnp.int32, (tm, tm), 1)
        before = jnp.where(c < r, 1.0, 0.0).astype(bf16)
        rank_s[...] = _dot(before, memb.astype(bf16))
        y_ref[...] = x_ref[...]

    lane = lax.broadcasted_iota(jnp.int32, (tm, LANES), 1)
    pick = lambda a: jnp.sum(jnp.where(lane == e, a, 0.0), axis=-1, keepdims=True)
    memb_e = pick(memb_s[...])
    rank_e = pick(rank_s[...])
    count = jnp.max(memb_e * (rank_e + 1.0))

    for blk in range(n_pass):
        rows = slice(blk * MOE_CAP, (blk + 1) * MOE_CAP)

        @pl.when(count > blk * MOE_CAP)
        def _():
            @pl.when(f == 0)
            def _():
                slot = lax.broadcasted_iota(jnp.int32, (tm, MOE_CAP), 1).astype(f32) + float(blk * MOE_CAP)
                ct = jnp.where((memb_e > 0.5) & (rank_e == slot), 1.0, 0.0).astype(bf16)
                ct_s[:, rows] = ct
                hc_s[rows, :] = _dot_tn(ct, h_s[...]).astype(bf16)

            hc = hc_s[rows, :]
            gg = _dot(hc, wg_ref[...])
            uu = _dot(hc, wu_ref[...])
            act = (gg * jax.nn.sigmoid(gg) * uu).astype(bf16)
            part = _dot(act, wd_ref[...])

            @pl.when(f == 0)
            def _():
                tmp_s[rows, :] = part

            @pl.when(f > 0)
            def _():
                tmp_s[rows, :] = tmp_s[rows, :] + part

            @pl.when(f == nf - 1)
            def _():
                yc = tmp_s[rows, :]
                hi = yc.astype(bf16)
                lo = (yc - hi.astype(f32)).astype(bf16)
                ct = ct_s[:, rows]
                ge = pick(gate_s[...])
                y_ref[...] = y_ref[...] + ge * (_dot(ct, hi) + _dot(ct, lo))

    if final_norm:
        @pl.when((e == ne - 1) & (f == nf - 1))
        def _():
            y_ref[...] = _rms(y_ref[...], gf_ref[...])


def _moe(x, g, router, w_gu, w_d, g_final, *, tf, final_norm):
    n, d = x.shape
    tm = MOE_TM
    ne = w_gu.shape[0]
    nf = D_FF // tf
    n_rows = -(-tm // MOE_CAP) * MOE_CAP
    full = lambda a: pl.BlockSpec(a.shape, lambda i, e, f: (0,) * a.ndim)
    return pl.pallas_call(
        functools.partial(_moe_kernel, final_norm=final_norm),
        grid=(n // tm, ne, nf),
        in_specs=[pl.BlockSpec((tm, d), lambda i, e, f: (i, 0)), full(g), full(router[0]), full(router[1]),
                  pl.BlockSpec((None, d, tf), lambda i, e, f: (e, 0, f)),
                  pl.BlockSpec((None, d, tf), lambda i, e, f: (e, 0, nf + f)),
                  pl.BlockSpec((None, tf, d), lambda i, e, f: (e, f, 0)),
                  full(g_final)],
        out_specs=pl.BlockSpec((tm, d), lambda i, e, f: (i, 0)),
        out_shape=jax.ShapeDtypeStruct((n, d), f32),
        scratch_shapes=[pltpu.VMEM((tm, d), bf16), pltpu.VMEM((tm, LANES), f32), pltpu.VMEM((tm, LANES), f32),
                        pltpu.VMEM((tm, LANES), f32), pltpu.VMEM((tm, n_rows), bf16),
                        pltpu.VMEM((n_rows, d), bf16), pltpu.VMEM((n_rows, d), f32)],
        compiler_params=pltpu.CompilerParams(dimension_semantics=("parallel", "arbitrary", "arbitrary"),
                                             vmem_limit_bytes=MOE_VMEM_LIMIT),
        name="moe_sparse",
    )(x, g, router[0], router[1], w_gu, w_gu, w_d, g_final)


def _lambda_init(layer):
    return 0.8 - 0.6 * math.exp(-0.3 * layer)


def _row(v, width=LANES, rows=8):
    out = jnp.zeros((rows, width), f32)
    for n, x in enumerate(v):
        x = jnp.asarray(x, f32).reshape(-1)
        out = out.at[n, :x.shape[0]].set(x)
    return out


def kernel(x_prompt, x_sample, cache_k, cache_v, cache_logf, cache_mem_k, cache_mem_v, state_wkv, state_shift,
           page_table, mem_prompt, norm_mix, w_in, b_f, diff_lam, diff_g, rw_mu, rw_w0, rw_w2, rw_a0, rw_a2, rw_g2,
           rw_kk, rw_ka, rw_rk, rw_ln_g, rw_ln_b, w_out, norm_mem, mem_norm, w_mem_q, w_mem_kv, w_mem_o, norm_ffn,
           ffn_gu, ffn_down, moe_router, moe_router_b, moe_gu, moe_down, norm_final):
    n_bp, t_p, d = x_prompt.shape
    n_bs, t_s, _ = x_sample.shape
    depth = w_in.shape[0]
    n_pages = page_table.shape[1]
    assert t_s == 4 and (n_pages * PAGE) % MOBA_BLOCK == 0 and n_pages % PAGES_PER_STEP == 0
    assert t_p % 512 == 0 and d == D_MODEL

    n_pool = cache_k.shape[1]
    k_t = jnp.transpose(cache_k, (0, 1, 3, 4, 2)).reshape(depth, n_pool, ATT_W, PAGE)
    v_t = jnp.transpose(cache_v, (0, 1, 3, 4, 2)).reshape(depth, n_pool, ATT_W, PAGE)
    lf_t = jnp.transpose(cache_logf, (0, 1, 3, 2))
    memk_t = jnp.transpose(cache_mem_k, (0, 1, 3, 4, 2)).reshape(depth, n_bs, GROUP_W, N_MEM)
    memv_t = jnp.transpose(cache_mem_v, (0, 1, 3, 4, 2)).reshape(depth, n_bs, GROUP_W, N_MEM)

    xp = x_prompt.reshape(n_bp * t_p, d)
    xs = x_sample.reshape(n_bs * t_s, d)
    zero_bias = jnp.zeros((1, LANES), f32)
    g_final = norm_final.reshape(1, d)
    outs = {k: [] for k in ("kp", "vp", "lfp", "mkp", "mvp", "wkvp", "shp", "ks", "vs", "lfs", "wkvs", "shs")}

    for l in range(depth):
        lam0 = _lambda_init(l)
        lp = diff_lam[l]
        lam = jnp.exp(jnp.sum(lp[0] * lp[1])) - jnp.exp(jnp.sum(lp[2] * lp[3])) + lam0
        g_pair = jnp.concatenate([diff_g[l], diff_g[l]])
        diff_par = _row([jnp.full((LANES,), lam), g_pair])
        w_l = w_in[l]
        w_cat = jnp.concatenate(
            [w_l[:, 3 * ATT_W + G_HEADS:], w_l[:, :3 * ATT_W],
             jnp.pad(w_l[:, 3 * ATT_W:3 * ATT_W + G_HEADS], ((0, 0), (0, LANES - G_HEADS)))], axis=1).astype(bf16)
        bias_pad = jnp.pad(b_f[l], (0, LANES - G_HEADS)).reshape(1, LANES)
        g_mix = norm_mix[l].reshape(1, d)
        w_out_b = w_out[l].astype(bf16)
        wq_b = w_mem_q[l].astype(bf16)
        wo_b = w_mem_o[l].astype(bf16)
        g_mem = norm_mem[l].reshape(1, d)
        g_ffn = norm_ffn[l].reshape(1, d)
        mu = rw_mu[l].reshape(1, RWKV_W)
        vecs = _row([rw_w0[l], rw_a0[l], rw_kk[l], rw_ka[l], rw_rk[l].reshape(-1)], width=GROUP_W)
        zpad = jnp.zeros((LANES - 64, GROUP_W), f32)
        w2p = jnp.concatenate([rw_w2[l], zpad], axis=0).astype(bf16)
        a2p = jnp.concatenate([zpad, rw_a2[l]], axis=0).astype(bf16)
        g2b = rw_g2[l].astype(bf16)
        ln = _row([rw_ln_g[l], rw_ln_b[l]], width=GROUP_W)
        last = l == depth - 1

        def rw_inputs(proj, n_b, t, shift0, tm):
            pr = proj[:, PR_COL:PR_COL + RWKV_W].reshape(n_b, t, RWKV_W)
            prev = jnp.concatenate([shift0[:, None, :], pr[:, :-1]], axis=1).reshape(n_b * t, RWKV_W)
            return pr, _rw_prep(proj, prev, mu, vecs, w2p, a2p, g2b, tm)

        def ffn(x2, tm):
            if l % 2 == 0:
                i = l // 2
                return _ffn(x2, g_ffn, ffn_gu[i:i + 1].astype(bf16), ffn_down[i:i + 1].astype(bf16), g_final,
                            tm=tm, tf=1408, final_norm=last)
            i = l // 2
            rt = jnp.pad(moe_router[i], ((0, 0), (0, LANES - N_EXPERTS)))
            rb = jnp.pad(moe_router_b[i], (0, LANES - N_EXPERTS)).reshape(1, LANES)
            if x2.shape[0] % MOE_TM == 0:
                return _moe(x2, g_ffn, (rt, rb), moe_gu[i].astype(bf16), moe_down[i].astype(bf16), g_final,
                            tf=1408, final_norm=last)
            return _ffn(x2, g_ffn, moe_gu[i].astype(bf16), moe_down[i].astype(bf16), g_final,
                        tm=tm, tf=1408, router=(rt, rb), final_norm=last)

        def wkv_to_bd(s):
            n = s.shape[0]
            m = jnp.zeros((n, GROUP_W, GROUP_W), f32)
            for h in range(G_HEADS):
                m = m.at[:, h * HEAD_DIM:(h + 1) * HEAD_DIM, h * HEAD_DIM:(h + 1) * HEAD_DIM].set(
                    jnp.swapaxes(s[:, h], 1, 2))
            return m

        def bd_to_wkv(m):
            return jnp.stack([jnp.swapaxes(m[:, h * HEAD_DIM:(h + 1) * HEAD_DIM, h * HEAD_DIM:(h + 1) * HEAD_DIM], 1, 2)
                              for h in range(G_HEADS)], axis=1)

        w_qkv_t = jnp.transpose(w_l[:, :3 * ATT_W]).astype(bf16)
        proj, q_t, k_t_out, v_t_out, k_b, v_tb = _norm_matmul(
            xp, g_mix, w_cat, bias_pad, tm=512, chunk=1152, logsig_tail=True, w_t=w_qkv_t, n_b=n_bp)
        pr, rw = rw_inputs(proj, n_bp, t_p, jnp.zeros((n_bp, RWKV_W), f32), 512)
        kaug = _cumsum_logf(proj, n_bp, t_p)
        g_col = jnp.broadcast_to(g_pair[:, None], (LANES, 128))
        o_a = _attn_prompt(k_b, q_t, v_tb, n_bp, t_p, "diff", 0, (diff_par, g_col), lam_scale=1.0 - lam0)
        o_b = _attn_prompt(k_b, q_t, v_tb, n_bp, t_p, "moba", 1)
        o_c = _attn_prompt(k_b, q_t, v_tb, n_bp, t_p, "fox", 2, kaug)
        o_d, m_fin = _rw_chunk(rw, jnp.zeros((n_bp, GROUP_W, GROUP_W), f32), ln, n_bp, t_p)
        kv_mem = _norm_matmul(mem_prompt.reshape(n_bp * N_MEM, d), mem_norm[l].reshape(1, d),
                              w_mem_kv[l].astype(bf16), zero_bias, tm=N_MEM, chunk=2 * GROUP_W, logsig_tail=False)
        mk_p = kv_mem[:, :GROUP_W].reshape(n_bp, N_MEM, GROUP_W)
        mv_p = kv_mem[:, GROUP_W:].reshape(n_bp, N_MEM, GROUP_W)
        to3 = lambda a: a.reshape(n_bp, t_p, a.shape[-1])
        x3 = _post_mixer(to3(xp), [to3(o_a), to3(o_b), to3(o_c), to3(o_d)], w_out_b, g_mem, wq_b,
                         jnp.swapaxes(mk_p, 1, 2), jnp.swapaxes(mv_p, 1, 2), wo_b, 512)
        xp = ffn(x3.reshape(n_bp * t_p, d), 512)
        from_t = lambda a: jnp.transpose(a.reshape(n_bp, 12, HEAD_DIM, t_p), (0, 3, 1, 2))
        outs["kp"].append(from_t(k_t_out))
        outs["vp"].append(from_t(v_t_out))
        outs["lfp"].append(proj[:, FL_COL:FL_COL + G_HEADS].reshape(n_bp, t_p, G_HEADS))
        outs["mkp"].append(mk_p.reshape(n_bp, N_MEM, G_HEADS, HEAD_DIM))
        outs["mvp"].append(mv_p.reshape(n_bp, N_MEM, G_HEADS, HEAD_DIM))
        outs["wkvp"].append(bd_to_wkv(m_fin))
        outs["shp"].append(pr[:, -1])

        proj = _norm_matmul(xs, g_mix, w_cat, bias_pad, tm=n_bs * t_s, chunk=1152, logsig_tail=True)
        pr, rw = rw_inputs(proj, n_bs, t_s, state_shift[l], n_bs * t_s)
        k_new = proj[:, K_COL:K_COL + ATT_W].reshape(n_bs, t_s, ATT_W)
        v_new = proj[:, V_COL:V_COL + ATT_W].reshape(n_bs, t_s, ATT_W)
        lf_new = proj[:, FL_COL:FL_COL + G_HEADS].reshape(n_bs, t_s, G_HEADS)
        pad_t = lambda a: jnp.pad(jnp.swapaxes(a, 1, 2), ((0, 0), (0, 0), (0, PAGE - t_s)))
        o_abc = _decode_attn(page_table, proj[:, Q_COL:Q_COL + ATT_W].reshape(n_bs, t_s, ATT_W), pad_t(k_new), pad_t(v_new),
                             pad_t(lf_new), diff_par, k_t, v_t, lf_t, l, 1.0 - lam0)
        t_pad = RW_CHUNK
        rw_pad = jnp.pad(rw.reshape(n_bs, t_s, 8 * GROUP_W), ((0, 0), (0, t_pad - t_s), (0, 0)))
        o_d, m_fin = _rw_chunk(rw_pad.reshape(n_bs * t_pad, 8 * GROUP_W), wkv_to_bd(state_wkv[l]), ln, n_bs, t_pad)
        o_d = o_d.reshape(n_bs, t_pad, GROUP_W)[:, :t_s]
        x3 = _post_mixer(xs.reshape(n_bs, t_s, d), [o_abc, o_d], w_out_b, g_mem, wq_b, memk_t[l], memv_t[l], wo_b, t_s)
        xs = ffn(x3.reshape(n_bs * t_s, d), n_bs * t_s)
        outs["ks"].append(k_new.reshape(n_bs, t_s, 12, HEAD_DIM))
        outs["vs"].append(v_new.reshape(n_bs, t_s, 12, HEAD_DIM))
        outs["lfs"].append(lf_new)
        outs["wkvs"].append(bd_to_wkv(m_fin))
        outs["shs"].append(pr[:, -1])

    st = lambda k: jnp.stack(outs[k])
    return (xp.reshape(n_bp, t_p, d), xs.reshape(n_bs, t_s, d), st("kp"), st("vp"), st("lfp"), st("mkp"), st("mvp"),
            st("wkvp"), st("shp"), st("ks"), st("vs"), st("lfs"), st("wkvs"), st("shs"))
```

```python
import functools
import math

import jax
import jax.numpy as jnp
from jax import lax
from jax.experimental import pallas as pl
from jax.experimental.pallas import tpu as pltpu

f32 = jnp.float32
bf16 = jnp.bfloat16

D_MODEL = 1024
HEAD_DIM = 64
G_HEADS = 4
GROUP_W = G_HEADS * HEAD_DIM
ATT_W = 3 * GROUP_W
RWKV_W = 1024
IN_W = 3 * ATT_W + G_HEADS + RWKV_W
PROJ_W = RWKV_W + 3 * ATT_W + 128
PR_COL = 0
Q_COL = RWKV_W
K_COL = Q_COL + ATT_W
V_COL = K_COL + ATT_W
FL_COL = V_COL + ATT_W
MOBA_BLOCK = 256
MOBA_TOPK = 3
PAGE = 128
N_MEM = 256
D_FF = 2816
N_EXPERTS = 8
NORM_EPS = 1e-6
GN_EPS = 64e-5
NEG = -1e30
LANES = 128
RW_CHUNK = 64
VMEM_LIMIT = 56 * 1024 * 1024
MOE_VMEM_LIMIT = 60 * 1024 * 1024

HI = lax.Precision.HIGHEST
LOG2E = 1.4426950408889634


def _cparams(*sem):
    return pltpu.CompilerParams(dimension_semantics=sem, vmem_limit_bytes=VMEM_LIMIT)


def _dot(a, b, prec=None):
    return jnp.dot(a, b, preferred_element_type=f32, precision=prec)


def _dot3(a, b):
    a_hi = a.astype(bf16)
    b_hi = b.astype(bf16)
    a_lo = (a - a_hi.astype(f32)).astype(bf16)
    b_lo = (b - b_hi.astype(f32)).astype(bf16)
    return _dot(a_hi, b_hi) + (_dot(a_hi, b_lo) + _dot(a_lo, b_hi))


def _dot_nt(a, b, prec=None):
    return lax.dot_general(a, b, (((1,), (1,)), ((), ())), preferred_element_type=f32, precision=prec)


def _dot_tn(a, b, prec=None):
    return lax.dot_general(a, b, (((0,), (0,)), ((), ())), preferred_element_type=f32, precision=prec)


def _rms(x, g):
    return x * lax.rsqrt(jnp.mean(x * x, axis=-1, keepdims=True) + NORM_EPS) * g


def _pair_rmsnorm(o, g_pair):
    lane = lax.broadcasted_iota(jnp.int32, o.shape, 1)
    lo = lane < HEAD_DIM
    sq = o * o
    s_lo = jnp.sum(jnp.where(lo, sq, 0.0), axis=-1, keepdims=True)
    s_hi = jnp.sum(jnp.where(lo, 0.0, sq), axis=-1, keepdims=True)
    ms = jnp.where(lo, s_lo, s_hi) * (1.0 / HEAD_DIM)
    return o * lax.rsqrt(ms + NORM_EPS) * g_pair


def _norm_matmul_kernel(*refs, chunk, logsig_tail, emit_t):
    x_ref, g_ref, w_ref, b_ref, o_ref = refs[:5]
    h = _rms(x_ref[...], g_ref[...]).astype(bf16)
    n_out = o_ref.shape[1]
    for c0 in range(0, n_out, chunk):
        c1 = min(c0 + chunk, n_out)
        y = _dot_nt(h, w_ref[c0:c1, :])
        if logsig_tail and c1 == n_out:
            t0 = c1 - c0 - LANES
            tail = jax.nn.log_sigmoid(y[:, t0:] + b_ref[...])
            o_ref[:, c0:c0 + t0] = y[:, :t0]
            o_ref[:, c0 + t0:c1] = tail
        else:
            o_ref[:, c0:c1] = y
    if emit_t:
        qt_ref, kt_ref, vt_ref, kb_ref, vtb_ref = refs[5:]
        qt_ref[...] = _dot_nt(w_ref[Q_COL:Q_COL + ATT_W, :], h)
        kt_ref[...] = _dot_nt(w_ref[K_COL:K_COL + ATT_W, :], h)
        vt = _dot_nt(w_ref[V_COL:V_COL + ATT_W, :], h)
        vt_ref[...] = vt
        vtb_ref[...] = vt.astype(bf16)
        kb_ref[...] = o_ref[:, K_COL:K_COL + ATT_W].astype(bf16)


def _norm_matmul(x, g, w_t, bias_pad, *, tm, chunk, logsig_tail, emit_t=False, n_b=1):
    n, d = x.shape
    n_out = w_t.shape[0]
    in_specs = [pl.BlockSpec((tm, d), lambda i: (i, 0)),
                pl.BlockSpec((1, d), lambda i: (0, 0)),
                pl.BlockSpec((n_out, d), lambda i: (0, 0)),
                pl.BlockSpec((1, LANES), lambda i: (0, 0))]
    out_specs = [pl.BlockSpec((tm, n_out), lambda i: (i, 0))]
    out_shape = [jax.ShapeDtypeStruct((n, n_out), f32)]
    if emit_t:
        t = n // n_b
        nt = t // tm
        t_spec = pl.BlockSpec((None, ATT_W, tm), lambda i: (i // nt, 0, i % nt))
        out_specs += [t_spec] * 3 + [pl.BlockSpec((tm, ATT_W), lambda i: (i, 0)), t_spec]
        out_shape += ([jax.ShapeDtypeStruct((n_b, ATT_W, t), f32)] * 3
                      + [jax.ShapeDtypeStruct((n, ATT_W), bf16), jax.ShapeDtypeStruct((n_b, ATT_W, t), bf16)])
    res = pl.pallas_call(
        functools.partial(_norm_matmul_kernel, chunk=chunk, logsig_tail=logsig_tail, emit_t=emit_t),
        grid=(n // tm,),
        in_specs=in_specs,
        out_specs=out_specs,
        out_shape=out_shape,
        compiler_params=_cparams("parallel"),
        name="norm_matmul",
    )(x, g, w_t, bias_pad)
    return res if emit_t else res[0]


def _cumsum_kernel(lf_ref, o_ref, carry_ref, *, tc):
    @pl.when(pl.program_id(1) == 0)
    def _():
        carry_ref[...] = jnp.zeros_like(carry_ref)
    r = lax.broadcasted_iota(jnp.int32, (tc, tc), 0)
    c = lax.broadcasted_iota(jnp.int32, (tc, tc), 1)
    tri = jnp.where(c <= r, 1.0, 0.0).astype(f32)
    cum = _dot(tri, lf_ref[...], HI) + carry_ref[...]
    carry_ref[...] = cum[tc - 1:tc, :]
    neg = cum * (-LOG2E)
    hi = neg.astype(bf16).astype(f32)
    mid = (neg - hi).astype(bf16).astype(f32)
    lo = (neg - hi - mid).astype(bf16).astype(f32)
    src = lax.broadcasted_iota(jnp.int32, (LANES, 2 * LANES), 0)
    dst = lax.broadcasted_iota(jnp.int32, (LANES, 2 * LANES), 1)
    out = jnp.zeros((tc, 2 * LANES), f32)
    for k, piece in enumerate((hi, mid, lo)):
        place = jnp.where((src < G_HEADS) & (dst == LANES * (src // 2) + 3 * (src % 2) + k), 1.0, 0.0)
        out = out + _dot(piece, place, HI)
    o_ref[...] = out.astype(bf16)


def _cumsum_logf(proj, n_b, t):
    tc = 256
    nt = t // tc
    return pl.pallas_call(
        functools.partial(_cumsum_kernel, tc=tc),
        grid=(n_b, nt),
        in_specs=[pl.BlockSpec((tc, LANES), lambda b, i: (b * nt + i, FL_COL // LANES))],
        out_specs=pl.BlockSpec((tc, 2 * LANES), lambda b, i: (b * nt + i, 0)),
        out_shape=jax.ShapeDtypeStruct((n_b * t, 2 * LANES), bf16),
        scratch_shapes=[pltpu.VMEM((1, LANES), f32)],
        compiler_params=_cparams("parallel", "arbitrary"),
        name="cumsum_logf",
    )(proj)


def _attn_kernel(*refs, mode, tq, tk, lam_scale):
    if mode == "diff":
        qt_ref, k_ref, vt_ref, par_ref, g_ref, o_ref = refs[:6]
    elif mode == "fox":
        qt_ref, k_ref, vt_ref, kaug_ref, o_ref = refs[:5]
    else:
        qt_ref, k_ref, vt_ref, o_ref = refs[:4]
    n_in = {"diff": 6, "fox": 5, "moba": 4}[mode]
    m_s, l_s, acc_s, sc_s = (refs[n_in:n_in + 2], refs[n_in + 2:n_in + 4], refs[n_in + 4:n_in + 8],
                             refs[n_in + 8:n_in + 10])
    if mode == "moba":
        kmean_s, sel_s = refs[n_in + 10], refs[n_in + 11:n_in + 13]
    i = pl.program_id(1)
    nvar = 4 if mode == "diff" else 2
    cols = nvar * tq
    drow = lax.broadcasted_iota(jnp.int32, (LANES, tq), 0)
    qpos = i * tq + (lax.broadcasted_iota(jnp.int32, (1, cols), 1) & (tq - 1))
    qs = []
    qs32 = []
    for hp in range(2):
        qt = qt_ref[hp * LANES:(hp + 1) * LANES, :]
        if mode == "diff":
            half = HEAD_DIM // 2
            scale = half ** -0.5 * LOG2E
            parts = [jnp.where((drow >= v * half) & (drow < (v + 1) * half), qt, 0.0) for v in range(4)]
        else:
            scale = HEAD_DIM ** -0.5 * LOG2E
            parts = [jnp.where(drow < HEAD_DIM, qt, 0.0), jnp.where(drow >= HEAD_DIM, qt, 0.0)]
        q32 = jnp.concatenate(parts, axis=1) * scale
        qs32.append(q32)
        if mode == "fox":
            r = lax.broadcasted_iota(jnp.int32, (LANES, cols), 0)
            c = lax.broadcasted_iota(jnp.int32, (LANES, cols), 1)
            ind = jnp.where(((r < 3) & (c < tq)) | ((r >= 3) & (r < 6) & (c >= tq)), 1.0, 0.0)
            qs.append(jnp.concatenate([q32, ind], axis=0).astype(bf16))
        else:
            qs.append(q32.astype(bf16))
        m_s[hp][...] = jnp.full(m_s[hp].shape, NEG, f32)
        l_s[hp][...] = jnp.zeros(l_s[hp].shape, f32)
        for hh in range(2):
            acc_s[2 * hp + hh][...] = jnp.zeros(acc_s[2 * hp + hh].shape, f32)

    if mode == "moba":
        nb = k_ref.shape[0] // MOBA_BLOCK

        @pl.when(i == 0)
        def _():
            kmean_s[...] = jnp.zeros(kmean_s.shape, f32)

            def blk(n, _):
                kb = k_ref[pl.ds(pl.multiple_of(n * MOBA_BLOCK, MOBA_BLOCK), MOBA_BLOCK), :]
                kmean_s[pl.ds(n, 1), :] = jnp.mean(kb.astype(f32), axis=0, keepdims=True)
                return 0
            lax.fori_loop(0, nb, blk, 0)

        nidx = lax.broadcasted_iota(jnp.int32, (LANES, cols), 0)
        nidx_f = nidx.astype(f32)
        own = qpos // MOBA_BLOCK
        own_f = own.astype(f32)
        for hp in range(2):
            gate = _dot(kmean_s[:, hp * LANES:(hp + 1) * LANES], qs32[hp], HI)
            g = jnp.where(nidx < own, gate, NEG)
            sel = jnp.zeros((LANES, cols), f32)
            for _ in range(MOBA_TOPK):
                mx = jnp.max(g, axis=0, keepdims=True)
                first = jnp.min(jnp.where(g == mx, nidx_f, 1e9), axis=0, keepdims=True)
                pick = nidx_f == first
                sel = jnp.where(pick & (first < own_f), 1.0, sel)
                g = jnp.where(pick, -3e38, g)
            sel_s[hp][...] = sel

    def scores(j, hp):
        start = pl.multiple_of(j * tk, tk)
        lanes = slice(hp * LANES, (hp + 1) * LANES)
        kb = k_ref[pl.ds(start, tk), lanes]
        if mode == "fox":
            kb = jnp.concatenate([kb, kaug_ref[pl.ds(start, tk), lanes]], axis=1)
        return _dot(kb, qs[hp])

    def absorb(s, j, hp, diagonal):
        start = pl.multiple_of(j * tk, tk)
        valid = None
        if mode == "moba" and not diagonal:
            valid = sel_s[hp][pl.ds(j, 1), :] > 0.5
        if diagonal:
            kpos = start + lax.broadcasted_iota(jnp.int32, (tk, 1), 0)
            valid = kpos <= qpos
        if valid is not None:
            s = jnp.where(valid, s, NEG)
        m_old = m_s[hp][...]
        m_new = jnp.maximum(m_old, jnp.max(s, axis=0, keepdims=True))
        alpha = jnp.exp2(m_old - m_new)
        p = jnp.exp2(s - m_new)
        if valid is not None:
            p = jnp.where(valid, p, 0.0)
        l_s[hp][...] = alpha * l_s[hp][...] + jnp.sum(p, axis=0, keepdims=True)
        pb = p.astype(bf16)
        hc = cols // 2
        for hh in range(2):
            vt = vt_ref[hp * LANES + hh * HEAD_DIM:hp * LANES + (hh + 1) * HEAD_DIM, pl.ds(start, tk)]
            acc = acc_s[2 * hp + hh]
            acc[...] = alpha[:, hh * hc:(hh + 1) * hc] * acc[...] + _dot(vt, pb[:, hh * hc:(hh + 1) * hc])
        m_s[hp][...] = m_new

    n_full = (i * tq) // tk
    for hp in range(2):
        sc_s[hp][...] = scores(0, hp)

    def body(j, _):
        for hp in range(2):
            nxt = scores(j + 1, hp)
            absorb(sc_s[hp][...], j, hp, False)
            sc_s[hp][...] = nxt
        return 0
    lax.fori_loop(0, n_full, body, 0)
    for hp in range(2):
        absorb(sc_s[hp][...], n_full, hp, True)

    hc = cols // 2
    for hp in range(2):
        inv_l = 1.0 / l_s[hp][...]
        a = [acc_s[2 * hp + hh][...] * inv_l[:, hh * hc:(hh + 1) * hc] for hh in range(2)]
        if mode == "diff":
            lam = par_ref[0:1, 0:1]
            o = jnp.concatenate([a_h[:, 0:tq] - lam * a_h[:, tq:2 * tq] for a_h in a], axis=0)
            lo = drow < HEAD_DIM
            sq = o * o
            s_lo = jnp.sum(jnp.where(lo, sq, 0.0), axis=0, keepdims=True)
            s_hi = jnp.sum(jnp.where(lo, 0.0, sq), axis=0, keepdims=True)
            ms = jnp.where(lo, s_lo, s_hi) * (1.0 / HEAD_DIM)
            o = o * lax.rsqrt(ms + NORM_EPS) * g_ref[...] * lam_scale
        else:
            o = jnp.concatenate(a, axis=0)
        o_ref[:, hp * LANES:(hp + 1) * LANES] = o.T


def _attn_prompt(k_b, q_t, v_t, n_b, t, mode, group, extra=None, lam_scale=1.0):
    tk = MOBA_BLOCK if mode == "moba" else 512
    tq = 128 if mode == "diff" else 256
    nq = t // tq
    nvar = 4 if mode == "diff" else 2
    cols = nvar * tq
    in_specs = [pl.BlockSpec((None, GROUP_W, tq), lambda b, i: (b, group, i)),
                pl.BlockSpec((t, GROUP_W), lambda b, i: (b, group)),
                pl.BlockSpec((None, GROUP_W, t), lambda b, i: (b, group, 0))]
    args = [q_t, k_b, v_t]
    scratch = ([pltpu.VMEM((1, cols), f32)] * 4 + [pltpu.VMEM((HEAD_DIM, cols // 2), f32)] * 4
               + [pltpu.VMEM((tk, cols), f32)] * 2)
    if mode == "diff":
        in_specs += [pl.BlockSpec((8, LANES), lambda b, i: (0, 0)),
                     pl.BlockSpec((LANES, tq), lambda b, i: (0, 0))]
        args += list(extra)
    elif mode == "fox":
        in_specs.append(pl.BlockSpec((t, GROUP_W), lambda b, i: (b, 0)))
        args.append(extra)
    else:
        scratch += [pltpu.VMEM((LANES, GROUP_W), f32)] + [pltpu.VMEM((LANES, cols), f32)] * 2
    return pl.pallas_call(
        functools.partial(_attn_kernel, mode=mode, tq=tq, tk=tk, lam_scale=lam_scale),
        grid=(n_b, nq),
        in_specs=in_specs,
        out_specs=pl.BlockSpec((tq, GROUP_W), lambda b, i: (b * nq + i, 0)),
        out_shape=jax.ShapeDtypeStruct((n_b * t, GROUP_W), f32),
        scratch_shapes=scratch,
        compiler_params=_cparams("parallel", "arbitrary"),
        name="attn_" + mode,
    )(*args)


def _head_ones(n):
    r = lax.broadcasted_iota(jnp.int32, (n, n), 0) // HEAD_DIM
    c = lax.broadcasted_iota(jnp.int32, (n, n), 1) // HEAD_DIM
    return jnp.where(r == c, 1.0, 0.0).astype(f32)


def _rw_prep_kernel(pr_ref, prev_ref, mu_ref, vec_ref, w2_ref, a2_ref, g2_ref, o_ref):
    pr = pr_ref[...]
    xm = pr + (prev_ref[...] - pr) * mu_ref[...]
    r = xm[:, 0:GROUP_W]
    k = xm[:, GROUP_W:2 * GROUP_W]
    v = xm[:, 2 * GROUP_W:3 * GROUP_W]
    x_lora = xm[:, 3 * GROUP_W:3 * GROUP_W + LANES]
    x_gate = xm[:, 3 * GROUP_W + LANES:]
    w0, a0, kk_p, ka, rk = (vec_ref[n:n + 1, :] for n in range(5))
    w_lin = w0 + _dot(jnp.tanh(x_lora).astype(bf16), w2_ref[...])
    w_log = -jax.nn.softplus(-w_lin) - 0.5
    lw = -jnp.exp(w_log)
    a = jax.nn.sigmoid(a0 + _dot(x_lora.astype(bf16), a2_ref[...]))
    g = _dot(jax.nn.sigmoid(x_gate).astype(bf16), g2_ref[...])
    ones = _head_ones(GROUP_W)
    kk = k * kk_p
    kk = kk * lax.rsqrt(_dot(kk * kk, ones, HI) + 1e-12)
    kh = k * (1.0 + (a - 1.0) * ka)
    bonus = _dot(r * kh * rk, ones, HI) * v
    for n, val in enumerate((r, lw, kh, v, kk, kk * a, g, bonus)):
        o_ref[:, n * GROUP_W:(n + 1) * GROUP_W] = val


def _rw_prep(proj, prev, mu, vecs, w2p, a2p, g2, tm):
    n = proj.shape[0]
    full = lambda shape: pl.BlockSpec(shape, lambda i: (0,) * len(shape))
    return pl.pallas_call(
        _rw_prep_kernel,
        grid=(n // tm,),
        in_specs=[pl.BlockSpec((tm, RWKV_W), lambda i: (i, PR_COL // RWKV_W)),
                  pl.BlockSpec((tm, RWKV_W), lambda i: (i, 0)),
                  full((1, RWKV_W)), full((8, GROUP_W)), full((LANES, GROUP_W)), full((LANES, GROUP_W)),
                  full((LANES, GROUP_W))],
        out_specs=pl.BlockSpec((tm, 8 * GROUP_W), lambda i: (i, 0)),
        out_shape=jax.ShapeDtypeStruct((n, 8 * GROUP_W), f32),
        compiler_params=_cparams("parallel"),
        name="rw_prep",
    )(proj, prev, mu, vecs, w2p, a2p, g2)


RW_STEP = 2


def _rw_chunk_kernel(x_ref, m0_ref, ln_ref, o_ref, mfin_ref, m_s, *, n_sub):
    c = pl.program_id(1)
    nc = pl.num_programs(1)
    C = RW_CHUNK
    W = GROUP_W

    @pl.when(c == 0)
    def _():
        m_s[...] = m0_ref[...]

    tr = lax.broadcasted_iota(jnp.int32, (C, C), 0)
    tc = lax.broadcasted_iota(jnp.int32, (C, C), 1)
    tri = jnp.where(tc <= tr, 1.0, 0.0).astype(f32)
    ri = lax.broadcasted_iota(jnp.int32, (W, W), 0)
    ci = lax.broadcasted_iota(jnp.int32, (W, W), 1)
    same = (ri // C) == (ci // HEAD_DIM)
    same_t = (ri // C) == (ci // C)
    strict = same_t & ((ci % C) < (ri % C))
    incl = same_t & ((ci % C) <= (ri % C))
    eye = jnp.where(ri == ci, 1.0, 0.0).astype(f32)
    mean_mat = _head_ones(W) * (1.0 / HEAD_DIM)

    def sm(x):
        return jnp.where(same, jnp.concatenate([x] * G_HEADS, axis=0), 0.0)

    def mm(x, y):
        return _dot(x.astype(bf16), y.astype(bf16))

    def chunk_ops(u):
        r, lw, kh, v, kk, b = (x_ref[u * C:(u + 1) * C, n * W:(n + 1) * W] for n in range(6))
        cum = _dot(tri, lw, HI)
        inv = jnp.exp(-cum)
        g_end = jnp.exp(cum[C - 1:C, :])
        a_sm = sm(-kk * jnp.exp(cum - lw))
        r_sm = sm(r * jnp.exp(cum))
        b_sm = sm(b * inv)
        k_sm = sm(kh * inv)
        v_sm = sm(v)
        gram = _dot_nt(jnp.concatenate([a_sm, r_sm], axis=0).astype(bf16),
                       jnp.concatenate([b_sm, k_sm], axis=0).astype(bf16))
        n_ab = jnp.where(strict, gram[:W, :W], 0.0)
        a_ak = jnp.where(strict, gram[:W, W:], 0.0)
        a_rb = jnp.where(incl, gram[W:, :W], 0.0)
        a_rk = jnp.where(incl, gram[W:, W:], 0.0)
        t_inv = eye + n_ab
        n_pow = n_ab
        for _ in range(int(math.log2(C)) - 1):
            n_pow = mm(n_pow, n_pow)
            t_inv = t_inv + mm(n_pow, t_inv)
        akv = mm(a_ak, v_sm)
        tx = mm(t_inv, jnp.concatenate([a_sm, akv], axis=1))
        y = mm(a_rb, tx)
        w_o = r_sm + y[:, :W]
        o_0 = y[:, W:] + mm(a_rk, v_sm)
        bt = (b_sm * g_end).astype(bf16)
        kt = (k_sm * g_end).astype(bf16)
        pq = _dot_tn(bt, tx.astype(bf16))
        p_mat = jnp.where(ri == ci, g_end, 0.0) + pq[:, :W]
        q_mat = pq[:, W:] + _dot_tn(kt, v_sm.astype(bf16))
        return w_o, o_0, p_mat, q_mat

    ops = [chunk_ops(u) for u in range(n_sub)]
    m = m_s[...]
    for u, (w_o, o_0, p_mat, q_mat) in enumerate(ops):
        o_sm = _dot3(w_o, m) + o_0
        m = _dot3(p_mat, m) + q_mat
        o = o_sm[0:C] + o_sm[C:2 * C] + o_sm[2 * C:3 * C] + o_sm[3 * C:4 * C]
        mu = _dot(o, mean_mat, HI)
        d = o - mu
        var = _dot(d * d, mean_mat, HI)
        o_n = d * lax.rsqrt(var + GN_EPS) * ln_ref[0:1, :] + ln_ref[1:2, :]
        g = x_ref[u * C:(u + 1) * C, 6 * W:7 * W]
        bonus = x_ref[u * C:(u + 1) * C, 7 * W:8 * W]
        o_ref[u * C:(u + 1) * C, :] = (o_n + bonus) * g
    m_s[...] = m

    @pl.when(c == nc - 1)
    def _():
        mfin_ref[...] = m


def _rw_chunk(rw, m0, ln, n_b, t):
    n_sub = RW_STEP if t % (RW_STEP * RW_CHUNK) == 0 else 1
    rows = n_sub * RW_CHUNK
    nc = t // rows
    return pl.pallas_call(
        functools.partial(_rw_chunk_kernel, n_sub=n_sub),
        grid=(n_b, nc),
        in_specs=[pl.BlockSpec((rows, 8 * GROUP_W), lambda b, c: (b * nc + c, 0)),
                  pl.BlockSpec((None, GROUP_W, GROUP_W), lambda b, c: (b, 0, 0)),
                  pl.BlockSpec((8, GROUP_W), lambda b, c: (0, 0))],
        out_specs=[pl.BlockSpec((rows, GROUP_W), lambda b, c: (b * nc + c, 0)),
                   pl.BlockSpec((None, GROUP_W, GROUP_W), lambda b, c: (b, 0, 0))],
        out_shape=[jax.ShapeDtypeStruct((n_b * t, GROUP_W), f32),
                   jax.ShapeDtypeStruct((n_b, GROUP_W, GROUP_W), f32)],
        scratch_shapes=[pltpu.VMEM((GROUP_W, GROUP_W), f32)],
        compiler_params=_cparams("parallel", "arbitrary"),
        name="rw_chunk",
    )(rw, m0, ln)


N_COLS = 64
COL_B, COL_C = 32, 48
PAGES_PER_STEP = 8


def _col_maps(shape, axis):
    c = lax.broadcasted_iota(jnp.int32, shape, axis)
    tok = c % 4
    head = jnp.where(c < COL_B, c // 8, jnp.where(c < COL_C, 4 + (c - COL_B) // 4, 8 + (c - COL_C) // 4))
    half = (c // 4) % 2
    lo = head * HEAD_DIM + jnp.where(c < COL_B, half * (HEAD_DIM // 2), 0)
    hi = jnp.where(c < COL_B, lo + HEAD_DIM // 2, lo + HEAD_DIM)
    return c, tok, head, lo, hi


def _decode_kernel(*refs, lam_scale, nj):
    npg = PAGES_PER_STEP
    pt_ref, q_ref, kn_ref, vn_ref, ln_ref, par_ref = refs[:6]
    k_refs, v_refs, l_refs = refs[6:6 + npg], refs[6 + npg:6 + 2 * npg], refs[6 + 2 * npg:6 + 3 * npg]
    o_ref, qbd_s, m_s, l_s, acc_s, carry_s, bm_s, bl_s, bg_s, bacc_s = refs[6 + 3 * npg:]
    j = pl.program_id(1)
    nblk = npg // 2
    c1, tok1, head1, lo1, hi1 = _col_maps((N_COLS, 1), 0)

    @pl.when(j == 0)
    def _():
        q4 = q_ref[...]
        lanes = lax.broadcasted_iota(jnp.int32, (N_COLS, ATT_W), 1)
        qb = jnp.zeros((N_COLS, ATT_W), f32)
        for t in range(4):
            qb = jnp.where(tok1 == t, q4[t:t + 1, :], qb)
        scale = jnp.where(c1 < COL_B, (HEAD_DIM // 2) ** -0.5, HEAD_DIM ** -0.5)
        qbd_s[...] = (jnp.where((lanes >= lo1) & (lanes < hi1), qb, 0.0) * scale).astype(bf16)
        m_s[...] = jnp.full(m_s.shape, NEG, f32)
        l_s[...] = jnp.zeros(l_s.shape, f32)
        acc_s[...] = jnp.zeros(acc_s.shape, f32)
        carry_s[...] = jnp.zeros(carry_s.shape, f32)
        bm_s[...] = jnp.zeros(bm_s.shape, f32)
        bl_s[...] = jnp.zeros(bl_s.shape, f32)
        bg_s[...] = jnp.zeros(bg_s.shape, f32)

    qbd = qbd_s[...]
    is_c = c1 >= COL_C

    def forget_bias(cum):
        bias = jnp.zeros((N_COLS, cum.shape[1]), f32)
        for h in range(G_HEADS):
            bias = jnp.where(is_c & (head1 == 8 + h), -cum[h:h + 1, :], bias)
        return bias

    def upper(n):
        r = lax.broadcasted_iota(jnp.int32, (n, n), 0)
        c = lax.broadcasted_iota(jnp.int32, (n, n), 1)
        return jnp.where(r <= c, 1.0, 0.0).astype(f32)

    def merge(parts):
        m_old = m_s[...]
        m_new = m_old
        for m_k, _, _ in parts:
            m_new = jnp.maximum(m_new, m_k)
        a = jnp.exp(m_old - m_new)
        l_new = a * l_s[...]
        acc_new = a * acc_s[...]
        for m_k, l_k, acc_k in parts:
            e_k = jnp.exp(m_k - m_new)
            l_new = l_new + e_k * l_k
            acc_new = acc_new + e_k * acc_k
        l_s[...] = l_new
        acc_s[...] = acc_new
        m_s[...] = m_new

    kw = 2 * PAGE
    lf = jnp.concatenate([jnp.concatenate([l_refs[2 * k][...], l_refs[2 * k + 1][...]], axis=1) for k in range(nblk)],
                         axis=0)
    cum_loc = _dot(lf, upper(kw), HI)
    carry = carry_s[0:G_HEADS, 0:1]
    blane = lax.broadcasted_iota(jnp.int32, (COL_C - COL_B, LANES), 1)
    parts = []
    for k in range(nblk):
        cum = cum_loc[G_HEADS * k:G_HEADS * (k + 1), :] + carry
        carry = cum[:, kw - 1:kw]
        s = jnp.concatenate([_dot(qbd, k_refs[2 * k][...].astype(bf16)),
                             _dot(qbd, k_refs[2 * k + 1][...].astype(bf16))], axis=1)
        gate = jnp.sum(s[COL_B:COL_C], axis=-1, keepdims=True) * (1.0 / MOBA_BLOCK)
        s = s + forget_bias(cum)
        m_blk = jnp.max(s, axis=-1, keepdims=True)
        p = jnp.exp(s - m_blk)
        l_blk = jnp.sum(p, axis=-1, keepdims=True)
        pb = p.astype(bf16)
        acc_blk = (_dot_nt(pb[:, :PAGE], v_refs[2 * k][...].astype(bf16))
                   + _dot_nt(pb[:, PAGE:], v_refs[2 * k + 1][...].astype(bf16)))
        parts.append((m_blk, l_blk, acc_blk))
        n = j * nblk + k
        bm_s[...] = jnp.where(blane == n, m_blk[COL_B:COL_C], bm_s[...])
        bl_s[...] = jnp.where(blane == n, l_blk[COL_B:COL_C], bl_s[...])
        bg_s[...] = jnp.where(blane == n, gate, bg_s[...])
        bacc_s[n] = acc_blk[COL_B:COL_C]
    carry_s[0:G_HEADS, :] = jnp.broadcast_to(carry, (G_HEADS, LANES))
    merge(parts)

    @pl.when(j == nj - 1)
    def _():
        sn = _dot(qbd, kn_ref[...].astype(bf16))
        cum_n = _dot(ln_ref[...], upper(PAGE), HI) + carry_s[0:G_HEADS, 0:1]
        klane = lax.broadcasted_iota(jnp.int32, (N_COLS, PAGE), 1)
        sn = jnp.where(klane <= tok1, sn + forget_bias(cum_n), NEG)
        m_n = jnp.max(sn, axis=-1, keepdims=True)
        p_n = jnp.exp(sn - m_n)
        l_n = jnp.sum(p_n, axis=-1, keepdims=True)
        acc_n = _dot_nt(p_n.astype(bf16), vn_ref[...].astype(bf16))
        merge([(m_n, l_n, acc_n)])

        nb = nj * nblk
        bgate = jnp.where(blane < nb, bg_s[...], NEG)
        blane_f = blane.astype(f32)
        sel = jnp.zeros(bgate.shape, f32)
        for _ in range(MOBA_TOPK):
            mx = jnp.max(bgate, axis=-1, keepdims=True)
            first = jnp.min(jnp.where(bgate == mx, blane_f, 1e9), axis=-1, keepdims=True)
            pick = blane_f == first
            sel = jnp.where(pick & (first < nb), 1.0, sel)
            bgate = jnp.where(pick, -3e38, bgate)
        on = sel > 0.5
        mb_n, lb_n, accb_n = m_n[COL_B:COL_C], l_n[COL_B:COL_C], acc_n[COL_B:COL_C]
        m_tot = jnp.maximum(jnp.max(jnp.where(on, bm_s[...], NEG), axis=-1, keepdims=True), mb_n)
        wgt = jnp.where(on, jnp.exp(bm_s[...] - m_tot), 0.0)
        e_n = jnp.exp(mb_n - m_tot)
        l_b = jnp.sum(wgt * bl_s[...], axis=-1, keepdims=True) + e_n * lb_n

        def add_block(n, acc):
            w_n = jnp.sum(jnp.where(blane == n, wgt, 0.0), axis=-1, keepdims=True)
            return acc + w_n * bacc_s[n]
        acc_b = lax.fori_loop(0, nb, add_block, e_n * accb_n)
        l_s[COL_B:COL_C, :] = l_b
        acc_s[COL_B:COL_C, :] = acc_b

        lam = par_ref[0:1, 0:1]
        coef = jnp.where((c1 < COL_B) & ((c1 // 4) % 2 == 1), -lam, 1.0) / l_s[...]
        lanes = lax.broadcasted_iota(jnp.int32, (N_COLS, ATT_W), 1)
        keep = (lanes >= head1 * HEAD_DIM) & (lanes < (head1 + 1) * HEAD_DIM)
        contrib = jnp.where(keep, acc_s[...] * coef, 0.0)
        _, tok_r, _, _, _ = _col_maps((8, N_COLS), 1)
        trow = lax.broadcasted_iota(jnp.int32, (8, N_COLS), 0)
        o_tok = _dot(jnp.where(tok_r == trow, 1.0, 0.0).astype(f32), contrib, HI)[0:4]
        o_ref[:, 2 * LANES:] = o_tok[:, 2 * LANES:]
        for hp in range(2):
            o_ref[:, hp * LANES:(hp + 1) * LANES] = (
                _pair_rmsnorm(o_tok[:, hp * LANES:(hp + 1) * LANES], par_ref[1:2, :]) * lam_scale)


def _decode_attn(page_table, q_s, kn_t, vn_t, ln_t, par, k_t, v_t, lf_t, layer, lam_scale):
    n_b, n_pages = page_table.shape
    npg = PAGES_PER_STEP
    nj = n_pages // npg
    nblocks = n_pages // 2

    def page_spec(rows, g):
        return pl.BlockSpec((None, None, rows, PAGE), lambda b, j, pt: (layer, pt[b, npg * j + g], 0, 0))
    new_spec = lambda rows: pl.BlockSpec((None, rows, PAGE), lambda b, j, pt: (b, 0, 0))
    nbc = COL_C - COL_B
    return pl.pallas_call(
        functools.partial(_decode_kernel, lam_scale=lam_scale, nj=nj),
        grid_spec=pltpu.PrefetchScalarGridSpec(
            num_scalar_prefetch=1,
            grid=(n_b, nj),
            in_specs=([pl.BlockSpec((None, 4, ATT_W), lambda b, j, pt: (b, 0, 0)),
                       new_spec(ATT_W), new_spec(ATT_W), new_spec(G_HEADS),
                       pl.BlockSpec((8, LANES), lambda b, j, pt: (0, 0))]
                      + [page_spec(ATT_W, g) for g in range(npg)] * 2
                      + [page_spec(G_HEADS, g) for g in range(npg)]),
            out_specs=pl.BlockSpec((None, 4, ATT_W), lambda b, j, pt: (b, 0, 0)),
            scratch_shapes=[pltpu.VMEM((N_COLS, ATT_W), bf16),
                            pltpu.VMEM((N_COLS, 1), f32), pltpu.VMEM((N_COLS, 1), f32),
                            pltpu.VMEM((N_COLS, ATT_W), f32), pltpu.VMEM((8, LANES), f32),
                            pltpu.VMEM((nbc, LANES), f32), pltpu.VMEM((nbc, LANES), f32),
                            pltpu.VMEM((nbc, LANES), f32), pltpu.VMEM((nblocks, nbc, ATT_W), f32)]),
        out_shape=jax.ShapeDtypeStruct((n_b, 4, ATT_W), f32),
        compiler_params=_cparams("parallel", "arbitrary"),
        name="decode_attn",
    )(page_table, q_s, kn_t, vn_t, ln_t, par, *([k_t] * npg), *([v_t] * npg), *([lf_t] * npg))


def _post_mixer_kernel(*refs, widths):
    n_o = len(widths)
    x_ref = refs[0]
    o_refs = refs[1:1 + n_o]
    w_out_ref, g_ref, wq_ref, mk_ref, mv_ref, wo_ref, y_ref = refs[1 + n_o:]
    x = x_ref[...]
    r0 = 0
    for o_r, w in zip(o_refs, widths):
        x = x + _dot(o_r[...].astype(bf16), w_out_ref[r0:r0 + w, :])
        r0 += w
    h = _rms(x, g_ref[...]).astype(bf16)
    q = _dot(h, wq_ref[...]) * (HEAD_DIM ** -0.5)
    lane = lax.broadcasted_iota(jnp.int32, q.shape, 1)
    mk = mk_ref[...].astype(bf16)
    mv = mv_ref[...].astype(bf16)
    o2 = jnp.zeros(q.shape, f32)
    for hd in range(G_HEADS):
        in_head = (lane >= hd * HEAD_DIM) & (lane < (hd + 1) * HEAD_DIM)
        s = _dot(jnp.where(in_head, q, 0.0).astype(bf16), mk)
        p = jnp.exp(s - jnp.max(s, axis=-1, keepdims=True))
        p = p / jnp.sum(p, axis=-1, keepdims=True)
        o2 = jnp.where(in_head, _dot_nt(p.astype(bf16), mv), o2)
    y_ref[...] = x + _dot(o2.astype(bf16), wo_ref[...])


def _post_mixer(x3, o_list, w_out, g, wq, mk_t, mv_t, wo, tm):
    n_b, t, d = x3.shape
    widths = tuple(o.shape[-1] for o in o_list)
    full = lambda a: pl.BlockSpec(a.shape, lambda b, i: (0,) * a.ndim)
    tok = lambda w: pl.BlockSpec((None, tm, w), lambda b, i: (b, i, 0))
    mem = pl.BlockSpec((None, GROUP_W, N_MEM), lambda b, i: (b, 0, 0))
    return pl.pallas_call(
        functools.partial(_post_mixer_kernel, widths=widths),
        grid=(n_b, t // tm),
        in_specs=[tok(d)] + [tok(w) for w in widths] + [full(w_out), full(g), full(wq), mem, mem, full(wo)],
        out_specs=tok(d),
        out_shape=jax.ShapeDtypeStruct(x3.shape, f32),
        compiler_params=_cparams("parallel", "parallel"),
        name="post_mixer",
    )(x3, *o_list, w_out, g, wq, mk_t, mv_t, wo)


def _ffn_kernel(*refs, moe, final_norm):
    if moe:
        x_ref, g_ref, rt_ref, rb_ref, wg_ref, wu_ref, wd_ref, gf_ref, y_ref, h_s, acc_s, tmp_s, gate_s = refs
    else:
        x_ref, g_ref, wg_ref, wu_ref, wd_ref, gf_ref, y_ref, h_s, acc_s = refs
    e = pl.program_id(1)
    f = pl.program_id(2)
    ne = pl.num_programs(1)
    nf = pl.num_programs(2)

    @pl.when((e == 0) & (f == 0))
    def _():
        h = _rms(x_ref[...], g_ref[...])
        h_s[...] = h.astype(bf16)
        acc_s[...] = jnp.zeros(acc_s.shape, f32)
        if moe:
            logits = _dot(h, rt_ref[...], HI) + rb_ref[...]
            lane = lax.broadcasted_iota(jnp.int32, logits.shape, 1)
            lane_f = lane.astype(f32)
            lg = jnp.where(lane < N_EXPERTS, logits, -3e38)
            v1 = jnp.max(lg, axis=-1, keepdims=True)
            i1 = jnp.min(jnp.where(lg == v1, lane_f, 1e9), axis=-1, keepdims=True)
            lg2 = jnp.where(lane_f == i1, -3e38, lg)
            v2 = jnp.max(lg2, axis=-1, keepdims=True)
            i2 = jnp.min(jnp.where(lg2 == v2, lane_f, 1e9), axis=-1, keepdims=True)
            e2 = jnp.exp(v2 - v1)
            den = 1.0 + e2
            gate_s[...] = jnp.where(lane_f == i1, 1.0 / den, jnp.where(lane_f == i2, e2 / den, 0.0))

    h = h_s[...]
    gg = _dot(h, wg_ref[...])
    uu = _dot(h, wu_ref[...])
    act = (gg * jax.nn.sigmoid(gg) * uu).astype(bf16)
    part = _dot(act, wd_ref[...])
    if moe:
        @pl.when(f == 0)
        def _():
            tmp_s[...] = part

        @pl.when(f > 0)
        def _():
            tmp_s[...] = tmp_s[...] + part

        @pl.when(f == nf - 1)
        def _():
            lane = lax.broadcasted_iota(jnp.int32, gate_s.shape, 1)
            ge = jnp.sum(jnp.where(lane == e, gate_s[...], 0.0), axis=-1, keepdims=True)
            acc_s[...] = acc_s[...] + ge * tmp_s[...]
    else:
        acc_s[...] = acc_s[...] + part

    @pl.when((e == ne - 1) & (f == nf - 1))
    def _():
        y = x_ref[...] + acc_s[...]
        if final_norm:
            y = _rms(y, gf_ref[...])
        y_ref[...] = y


def _ffn(x, g, w_gu, w_d, g_final, *, tm, tf, router=None, final_norm=False):
    n, d = x.shape
    moe = router is not None
    ne = w_gu.shape[0]
    nf = D_FF // tf
    full = lambda a: pl.BlockSpec(a.shape, lambda i, e, f: (0,) * a.ndim)
    in_specs = [pl.BlockSpec((tm, d), lambda i, e, f: (i, 0)), full(g)]
    args = [x, g]
    if moe:
        in_specs += [full(router[0]), full(router[1])]
        args += list(router)
    in_specs += [pl.BlockSpec((None, d, tf), lambda i, e, f: (e, 0, f)),
                 pl.BlockSpec((None, d, tf), lambda i, e, f: (e, 0, nf + f)),
                 pl.BlockSpec((None, tf, d), lambda i, e, f: (e, f, 0)),
                 full(g_final)]
    args += [w_gu, w_gu, w_d, g_final]
    scratch = [pltpu.VMEM((tm, d), bf16), pltpu.VMEM((tm, d), f32)]
    if moe:
        scratch += [pltpu.VMEM((tm, d), f32), pltpu.VMEM((tm, LANES), f32)]
    return pl.pallas_call(
        functools.partial(_ffn_kernel, moe=moe, final_norm=final_norm),
        grid=(n // tm, ne, nf),
        in_specs=in_specs,
        out_specs=pl.BlockSpec((tm, d), lambda i, e, f: (i, 0)),
        out_shape=jax.ShapeDtypeStruct((n, d), f32),
        scratch_shapes=scratch,
        compiler_params=_cparams("parallel", "arbitrary", "arbitrary"),
        name="ffn_moe" if moe else "ffn_dense",
    )(*args)


MOE_TM = 1024
MOE_PASSES = (256, 128, 384, 256)


def _moe_kernel(x_ref, g_ref, rt_ref, rb_ref, wg_ref, wu_ref, wd_ref, gf_ref, y_ref,
                h_s, gate_s, memb_s, rank_s, ct_s, hc_s, tmp_s, *, final_norm):
    e = pl.program_id(1)
    f = pl.program_id(2)
    ne = pl.num_programs(1)
    nf = pl.num_programs(2)
    tm = x_ref.shape[0]

    @pl.when((e == 0) & (f == 0))
    def _():
        h = _rms(x_ref[...], g_ref[...])
        h_s[...] = h.astype(bf16)
        logits = _dot(h, rt_ref[...], HI) + rb_ref[...]
        lane_f = lax.broadcasted_iota(jnp.int32, logits.shape, 1).astype(f32)
        lg = jnp.where(lane_f < N_EXPERTS, logits, -3e38)
        v1 = jnp.max(lg, axis=-1, keepdims=True)
        i1 = jnp.min(jnp.where(lg == v1, lane_f, 1e9), axis=-1, keepdims=True)
        lg2 = jnp.where(lane_f == i1, -3e38, lg)
        v2 = jnp.max(lg2, axis=-1, keepdims=True)
        i2 = jnp.min(jnp.where(lg2 == v2, lane_f, 1e9), axis=-1, keepdims=True)
        e2 = jnp.exp(v2 - v1)
        den = 1.0 + e2
        gate_s[...] = jnp.where(lane_f == i1, 1.0 / den, jnp.where(lane_f == i2, e2 / den, 0.0))
        memb = jnp.where((lane_f == i1) | (lane_f == i2), 1.0, 0.0)
        memb_s[...] = memb
        r = lax.broadcasted_iota(jnp.int32, (tm, tm), 0)
        c = lax.broadcasted_iota(jnp.int32, (tm, tm), 1)
        before = jnp.where(c < r, 1.0, 0.0).astype(bf16)
        rank_s[...] = _dot(before, memb.astype(bf16))
        y_ref[...] = x_ref[...]

    lane = lax.broadcasted_iota(jnp.int32, (tm, LANES), 1)
    pick = lambda a: jnp.sum(jnp.where(lane == e, a, 0.0), axis=-1, keepdims=True)
    memb_e = pick(memb_s[...])
    rank_e = pick(rank_s[...])
    count = jnp.max(memb_e * (rank_e + 1.0))

    first = 0
    for cap in MOE_PASSES:
        rows = slice(first, first + cap)
        run = count > first
        base = float(first)
        first += cap

        @pl.when(run)
        def _(rows=rows, cap=cap, base=base):
            @pl.when(f == 0)
            def _():
                slot = lax.broadcasted_iota(jnp.int32, (tm, cap), 1).astype(f32) + base
                ct = jnp.where((memb_e > 0.5) & (rank_e == slot), 1.0, 0.0).astype(bf16)
                ct_s[:, rows] = ct
                hc_s[rows, :] = _dot_tn(ct, h_s[...]).astype(bf16)

            hc = hc_s[rows, :]
            gg = _dot(hc, wg_ref[...])
            uu = _dot(hc, wu_ref[...])
            act = (gg * jax.nn.sigmoid(gg) * uu).astype(bf16)
            part = _dot(act, wd_ref[...])

            @pl.when(f == 0)
            def _():
                tmp_s[rows, :] = part

            @pl.when(f > 0)
            def _():
                tmp_s[rows, :] = tmp_s[rows, :] + part

            @pl.when(f == nf - 1)
            def _():
                yc = tmp_s[rows, :]
                hi = yc.astype(bf16)
                lo = (yc - hi.astype(f32)).astype(bf16)
                ct = ct_s[:, rows]
                ge = pick(gate_s[...])
                y_ref[...] = y_ref[...] + ge * (_dot(ct, hi) + _dot(ct, lo))

    if final_norm:
        @pl.when((e == ne - 1) & (f == nf - 1))
        def _():
            y_ref[...] = _rms(y_ref[...], gf_ref[...])


def _moe(x, g, router, w_gu, w_d, g_final, *, tf, final_norm):
    n, d = x.shape
    tm = MOE_TM
    ne = w_gu.shape[0]
    nf = D_FF // tf
    n_rows = sum(MOE_PASSES)
    assert n_rows >= tm
    full = lambda a: pl.BlockSpec(a.shape, lambda i, e, f: (0,) * a.ndim)
    return pl.pallas_call(
        functools.partial(_moe_kernel, final_norm=final_norm),
        grid=(n // tm, ne, nf),
        in_specs=[pl.BlockSpec((tm, d), lambda i, e, f: (i, 0)), full(g), full(router[0]), full(router[1]),
                  pl.BlockSpec((None, d, tf), lambda i, e, f: (e, 0, f)),
                  pl.BlockSpec((None, d, tf), lambda i, e, f: (e, 0, nf + f)),
                  pl.BlockSpec((None, tf, d), lambda i, e, f: (e, f, 0)),
                  full(g_final)],
        out_specs=pl.BlockSpec((tm, d), lambda i, e, f: (i, 0)),
        out_shape=jax.ShapeDtypeStruct((n, d), f32),
        scratch_shapes=[pltpu.VMEM((tm, d), bf16), pltpu.VMEM((tm, LANES), f32), pltpu.VMEM((tm, LANES), f32),
                        pltpu.VMEM((tm, LANES), f32), pltpu.VMEM((tm, n_rows), bf16),
                        pltpu.VMEM((n_rows, d), bf16), pltpu.VMEM((n_rows, d), f32)],
        compiler_params=pltpu.CompilerParams(dimension_semantics=("parallel", "arbitrary", "arbitrary"),
                                             vmem_limit_bytes=MOE_VMEM_LIMIT),
        name="moe_sparse",
    )(x, g, router[0], router[1], w_gu, w_gu, w_d, g_final)


def _lambda_init(layer):
    return 0.8 - 0.6 * math.exp(-0.3 * layer)


def _row(v, width=LANES, rows=8):
    out = jnp.zeros((rows, width), f32)
    for n, x in enumerate(v):
        x = jnp.asarray(x, f32).reshape(-1)
        out = out.at[n, :x.shape[0]].set(x)
    return out


def kernel(x_prompt, x_sample, cache_k, cache_v, cache_logf, cache_mem_k, cache_mem_v, state_wkv, state_shift,
           page_table, mem_prompt, norm_mix, w_in, b_f, diff_lam, diff_g, rw_mu, rw_w0, rw_w2, rw_a0, rw_a2, rw_g2,
           rw_kk, rw_ka, rw_rk, rw_ln_g, rw_ln_b, w_out, norm_mem, mem_norm, w_mem_q, w_mem_kv, w_mem_o, norm_ffn,
           ffn_gu, ffn_down, moe_router, moe_router_b, moe_gu, moe_down, norm_final):
    n_bp, t_p, d = x_prompt.shape
    n_bs, t_s, _ = x_sample.shape
    depth = w_in.shape[0]
    n_pages = page_table.shape[1]
    assert t_s == 4 and (n_pages * PAGE) % MOBA_BLOCK == 0 and n_pages % PAGES_PER_STEP == 0
    assert t_p % 512 == 0 and d == D_MODEL

    n_pool = cache_k.shape[1]
    k_t = jnp.transpose(cache_k, (0, 1, 3, 4, 2)).reshape(depth, n_pool, ATT_W, PAGE)
    v_t = jnp.transpose(cache_v, (0, 1, 3, 4, 2)).reshape(depth, n_pool, ATT_W, PAGE)
    lf_t = jnp.transpose(cache_logf, (0, 1, 3, 2))
    memk_t = jnp.transpose(cache_mem_k, (0, 1, 3, 4, 2)).reshape(depth, n_bs, GROUP_W, N_MEM)
    memv_t = jnp.transpose(cache_mem_v, (0, 1, 3, 4, 2)).reshape(depth, n_bs, GROUP_W, N_MEM)

    w_in_t = jnp.transpose(w_in, (2, 0, 1))
    xp = x_prompt.reshape(n_bp * t_p, d)
    xs = x_sample.reshape(n_bs * t_s, d)
    zero_bias = jnp.zeros((1, LANES), f32)
    g_final = norm_final.reshape(1, d)
    outs = {k: [] for k in ("kp", "vp", "lfp", "mkp", "mvp", "wkvp", "shp", "ks", "vs", "lfs", "wkvs", "shs")}

    for l in range(depth):
        lam0 = _lambda_init(l)
        lp = diff_lam[l]
        lam = jnp.exp(jnp.sum(lp[0] * lp[1])) - jnp.exp(jnp.sum(lp[2] * lp[3])) + lam0
        g_pair = jnp.concatenate([diff_g[l], diff_g[l]])
        diff_par = _row([jnp.full((LANES,), lam), g_pair])
        w_rows = w_in_t[:, l, :]
        w_cat = jnp.concatenate(
            [w_rows[3 * ATT_W + G_HEADS:], w_rows[:3 * ATT_W],
             jnp.pad(w_rows[3 * ATT_W:3 * ATT_W + G_HEADS], ((0, LANES - G_HEADS), (0, 0)))], axis=0).astype(bf16)
        bias_pad = jnp.pad(b_f[l], (0, LANES - G_HEADS)).reshape(1, LANES)
        g_mix = norm_mix[l].reshape(1, d)
        w_out_b = w_out[l].astype(bf16)
        wq_b = w_mem_q[l].astype(bf16)
        wo_b = w_mem_o[l].astype(bf16)
        g_mem = norm_mem[l].reshape(1, d)
        g_ffn = norm_ffn[l].reshape(1, d)
        mu = rw_mu[l].reshape(1, RWKV_W)
        vecs = _row([rw_w0[l], rw_a0[l], rw_kk[l], rw_ka[l], rw_rk[l].reshape(-1)], width=GROUP_W)
        zpad = jnp.zeros((LANES - 64, GROUP_W), f32)
        w2p = jnp.concatenate([rw_w2[l], zpad], axis=0).astype(bf16)
        a2p = jnp.concatenate([zpad, rw_a2[l]], axis=0).astype(bf16)
        g2b = rw_g2[l].astype(bf16)
        ln = _row([rw_ln_g[l], rw_ln_b[l]], width=GROUP_W)
        last = l == depth - 1

        def rw_inputs(proj, n_b, t, shift0, tm):
            pr = proj[:, PR_COL:PR_COL + RWKV_W].reshape(n_b, t, RWKV_W)
            prev = jnp.concatenate([shift0[:, None, :], pr[:, :-1]], axis=1).reshape(n_b * t, RWKV_W)
            return pr, _rw_prep(proj, prev, mu, vecs, w2p, a2p, g2b, tm)

        def ffn(x2, tm):
            if l % 2 == 0:
                i = l // 2
                return _ffn(x2, g_ffn, ffn_gu[i:i + 1].astype(bf16), ffn_down[i:i + 1].astype(bf16), g_final,
                            tm=tm, tf=1408, final_norm=last)
            i = l // 2
            rt = jnp.pad(moe_router[i], ((0, 0), (0, LANES - N_EXPERTS)))
            rb = jnp.pad(moe_router_b[i], (0, LANES - N_EXPERTS)).reshape(1, LANES)
            if x2.shape[0] % MOE_TM == 0:
                return _moe(x2, g_ffn, (rt, rb), moe_gu[i].astype(bf16), moe_down[i].astype(bf16), g_final,
                            tf=1408, final_norm=last)
            return _ffn(x2, g_ffn, moe_gu[i].astype(bf16), moe_down[i].astype(bf16), g_final,
                        tm=tm, tf=1408, router=(rt, rb), final_norm=last)

        def wkv_to_bd(s):
            n = s.shape[0]
            m = jnp.zeros((n, GROUP_W, GROUP_W), f32)
            for h in range(G_HEADS):
                m = m.at[:, h * HEAD_DIM:(h + 1) * HEAD_DIM, h * HEAD_DIM:(h + 1) * HEAD_DIM].set(
                    jnp.swapaxes(s[:, h], 1, 2))
            return m

        def bd_to_wkv(m):
            return jnp.stack([jnp.swapaxes(m[:, h * HEAD_DIM:(h + 1) * HEAD_DIM, h * HEAD_DIM:(h + 1) * HEAD_DIM], 1, 2)
                              for h in range(G_HEADS)], axis=1)

        proj, q_t, k_t_out, v_t_out, k_b, v_tb = _norm_matmul(
            xp, g_mix, w_cat, bias_pad, tm=512, chunk=1152, logsig_tail=True, emit_t=True, n_b=n_bp)
        pr, rw = rw_inputs(proj, n_bp, t_p, jnp.zeros((n_bp, RWKV_W), f32), 512)
        kaug = _cumsum_logf(proj, n_bp, t_p)
        g_col = jnp.broadcast_to(g_pair[:, None], (LANES, 128))
        o_a = _attn_prompt(k_b, q_t, v_tb, n_bp, t_p, "diff", 0, (diff_par, g_col), lam_scale=1.0 - lam0)
        o_b = _attn_prompt(k_b, q_t, v_tb, n_bp, t_p, "moba", 1)
        o_c = _attn_prompt(k_b, q_t, v_tb, n_bp, t_p, "fox", 2, kaug)
        o_d, m_fin = _rw_chunk(rw, jnp.zeros((n_bp, GROUP_W, GROUP_W), f32), ln, n_bp, t_p)
        kv_mem = _norm_matmul(mem_prompt.reshape(n_bp * N_MEM, d), mem_norm[l].reshape(1, d),
                              jnp.transpose(w_mem_kv[l]).astype(bf16), zero_bias, tm=N_MEM, chunk=2 * GROUP_W, logsig_tail=False)
        mk_p = kv_mem[:, :GROUP_W].reshape(n_bp, N_MEM, GROUP_W)
        mv_p = kv_mem[:, GROUP_W:].reshape(n_bp, N_MEM, GROUP_W)
        to3 = lambda a: a.reshape(n_bp, t_p, a.shape[-1])
        x3 = _post_mixer(to3(xp), [to3(o_a), to3(o_b), to3(o_c), to3(o_d)], w_out_b, g_mem, wq_b,
                         jnp.swapaxes(mk_p, 1, 2), jnp.swapaxes(mv_p, 1, 2), wo_b, 512)
        xp = ffn(x3.reshape(n_bp * t_p, d), 512)
        from_t = lambda a: jnp.transpose(a.reshape(n_bp, 12, HEAD_DIM, t_p), (0, 3, 1, 2))
        outs["kp"].append(from_t(k_t_out))
        outs["vp"].append(from_t(v_t_out))
        outs["lfp"].append(proj[:, FL_COL:FL_COL + G_HEADS].reshape(n_bp, t_p, G_HEADS))
        outs["mkp"].append(mk_p.reshape(n_bp, N_MEM, G_HEADS, HEAD_DIM))
        outs["mvp"].append(mv_p.reshape(n_bp, N_MEM, G_HEADS, HEAD_DIM))
        outs["wkvp"].append(bd_to_wkv(m_fin))
        outs["shp"].append(pr[:, -1])

        proj = _norm_matmul(xs, g_mix, w_cat, bias_pad, tm=n_bs * t_s, chunk=1152, logsig_tail=True)
        pr, rw = rw_inputs(proj, n_bs, t_s, state_shift[l], n_bs * t_s)
        k_new = proj[:, K_COL:K_COL + ATT_W].reshape(n_bs, t_s, ATT_W)
        v_new = proj[:, V_COL:V_COL + ATT_W].reshape(n_bs, t_s, ATT_W)
        lf_new = proj[:, FL_COL:FL_COL + G_HEADS].reshape(n_bs, t_s, G_HEADS)
        pad_t = lambda a: jnp.pad(jnp.swapaxes(a, 1, 2), ((0, 0), (0, 0), (0, PAGE - t_s)))
        o_abc = _decode_attn(page_table, proj[:, Q_COL:Q_COL + ATT_W].reshape(n_bs, t_s, ATT_W), pad_t(k_new), pad_t(v_new),
                             pad_t(lf_new), diff_par, k_t, v_t, lf_t, l, 1.0 - lam0)
        t_pad = RW_CHUNK
        rw_pad = jnp.pad(rw.reshape(n_bs, t_s, 8 * GROUP_W), ((0, 0), (0, t_pad - t_s), (0, 0)))
        o_d, m_fin = _rw_chunk(rw_pad.reshape(n_bs * t_pad, 8 * GROUP_W), wkv_to_bd(state_wkv[l]), ln, n_bs, t_pad)
        o_d = o_d.reshape(n_bs, t_pad, GROUP_W)[:, :t_s]
        x3 = _post_mixer(xs.reshape(n_bs, t_s, d), [o_abc, o_d], w_out_b, g_mem, wq_b, memk_t[l], memv_t[l], wo_b, t_s)
        xs = ffn(x3.reshape(n_bs * t_s, d), n_bs * t_s)
        outs["ks"].append(k_new.reshape(n_bs, t_s, 12, HEAD_DIM))
        outs["vs"].append(v_new.reshape(n_bs, t_s, 12, HEAD_DIM))
        outs["lfs"].append(lf_new)
        outs["wkvs"].append(bd_to_wkv(m_fin))
        outs["shs"].append(pr[:, -1])

    st = lambda k: jnp.stack(outs[k])
    return (xp.reshape(n_bp, t_p, d), xs.reshape(n_bs, t_s, d), st("kp"), st("vp"), st("lfp"), st("mkp"), st("mvp"),
            st("wkvp"), st("shp"), st("ks"), st("vs"), st("lfs"), st("wkvs"), st("shs"))
```
